```python
import math
import jax, jax.numpy as jnp
from jax import lax
import numpy as np

D_MODEL = 2048
BATCH = 16
SEQ = 2048
DEPTH = 2

D_BRANCH = 1024
N_BRANCH = 3
POOL_GROUPS = 4
POOL_WINDOWS = (2, 4, 8, 16)
POOL_GROUP_DIM = D_BRANCH // POOL_GROUPS
SB_HEADS = 8
SB_HEAD_DIM = D_BRANCH // SB_HEADS
SB_BLOCK = 128
ML_HEADS = 4
ML_HEAD_DIM = D_BRANCH // ML_HEADS
ML_CHUNK = 64
ML_CONV = 4
D_FF = 5632
FFN_CONV = 3
ALPHA = (2.0 * DEPTH) ** 0.25
BETA = (8.0 * DEPTH) ** -0.25
LN_EPS = 1e-5

SPLIT_SIZES = (D_BRANCH, D_BRANCH, D_BRANCH, D_BRANCH, 2 * D_BRANCH, D_BRANCH, D_BRANCH,
               ML_HEADS, ML_HEADS, N_BRANCH * D_MODEL)
D_IN = sum(SPLIT_SIZES)
SPLIT_POINTS = tuple(int(p) for p in np.cumsum(SPLIT_SIZES)[:-1])

kernel_name = "hybrid_pool_stickbreak_mlstm_convffn_deepnorm"


def layer_norm(x, g, b):
    xf = x.astype(jnp.float32)
    mu = jnp.mean(xf, axis=-1, keepdims=True)
    var = jnp.mean(jnp.square(xf - mu), axis=-1, keepdims=True)
    y = (xf - mu) * lax.rsqrt(var + LN_EPS)
    return (y * g.astype(jnp.float32) + b.astype(jnp.float32)).astype(x.dtype)


def causal_dwconv(x, w):
    K = w.shape[0]
    S = x.shape[1]
    xp = jnp.pad(x, ((0, 0), (K - 1, 0), (0, 0)))
    out = xp[:, 0:S] * w[0]
    for k in range(1, K):
        out = out + xp[:, k:k + S] * w[k]
    return out


def pool_mixer(a, w_grp, scale):
    B, S, _ = a.shape
    a_g = a.reshape(B, S, POOL_GROUPS, POOL_GROUP_DIM).astype(jnp.float32)
    cs = jnp.cumsum(a_g, axis=1)
    t = jnp.arange(S)
    pooled = []
    for g, w in enumerate(POOL_WINDOWS):
        cs_g = cs[:, :, g]
        prev = jnp.pad(cs_g, ((0, 0), (w, 0), (0, 0)))[:, :S]
        cnt = jnp.minimum(t + 1, w).astype(jnp.float32)[None, :, None]
        pooled.append((cs_g - prev) / cnt)
    diff = (jnp.stack(pooled, axis=2) - a_g).astype(a.dtype)
    mixed = jnp.einsum('bsgc,gcd->bsgd', diff, w_grp)
    return mixed.reshape(B, S, D_BRANCH) * scale


def stick_breaking_attention(q, k, v):
    B, S, H, Dh = q.shape
    scale = Dh ** -0.5
    outs = []
    for blk in range(S // SB_BLOCK):
        start = blk * SB_BLOCK
        end = start + SB_BLOCK
        qb = q[:, start:end]
        kb = k[:, :end]
        vb = v[:, :end]
        z = jnp.einsum('bqhd,bkhd->bhqk', qb, kb).astype(jnp.float32) * scale
        t_idx = start + jnp.arange(SB_BLOCK)
        s_idx = jnp.arange(end)
        mask = s_idx[None, :] < t_idx[:, None]
        log_beta = jax.nn.log_sigmoid(z)
        log_keep = jnp.where(mask, jax.nn.log_sigmoid(-z), 0.0)
        after = lax.cumsum(log_keep, axis=3, reverse=True) - log_keep
        attn = jnp.where(mask, jnp.exp(log_beta + after), 0.0)
        outs.append(jnp.einsum('bhqk,bkhd->bqhd', attn.astype(v.dtype), vb))
    return jnp.concatenate(outs, axis=1)


def _to_chunks(x):
    B, S, H, D = x.shape
    return x.reshape(B, S // ML_CHUNK, ML_CHUNK, H, D).transpose(1, 0, 3, 2, 4)


def _gate_chunks(g):
    B, S, H = g.shape
    return g.reshape(B, S // ML_CHUNK, ML_CHUNK, H).transpose(1, 0, 3, 2)


def mlstm(q, k, v, i_pre, f_pre):
    B, S, H, Dh = q.shape
    f32 = jnp.float32
    qc = _to_chunks(q.astype(f32))
    kc = _to_chunks(k.astype(f32) * (Dh ** -0.5))
    vc = _to_chunks(v.astype(f32))
    ic = _gate_chunks(i_pre.astype(f32))
    fc = _gate_chunks(jax.nn.log_sigmoid(f_pre.astype(f32)))
    causal = jnp.tril(jnp.ones((ML_CHUNK, ML_CHUNK), dtype=bool))

    def step(carry, inp):
        C, n, m = carry
        qj, kj, vj, ij, fj = inp
        b = jnp.cumsum(fj, axis=-1)
        D = b[..., :, None] - b[..., None, :] + ij[..., None, :]
        D = jnp.where(causal, D, -jnp.inf)
        inter = b + m[..., None]
        m_t = jnp.maximum(inter, jnp.max(D, axis=-1))
        w_intra = jnp.exp(D - m_t[..., None])
        w_inter = jnp.exp(inter - m_t)
        qk = jnp.einsum('bhtd,bhsd->bhts', qj, kj) * w_intra
        num = (w_inter[..., None] * jnp.einsum('bhvk,bhtk->bhtv', C, qj)
               + jnp.einsum('bhts,bhsv->bhtv', qk, vj))
        den = w_inter * jnp.einsum('bhk,bhtk->bht', n, qj) + jnp.sum(qk, axis=-1)
        h = num / jnp.maximum(jnp.abs(den), jnp.exp(-m_t))[..., None]
        b_L = b[..., -1]
        dec = b_L[..., None] - b + ij
        m_new = jnp.maximum(b_L + m, jnp.max(dec, axis=-1))
        w_s = jnp.exp(dec - m_new[..., None])
        w_prev = jnp.exp(b_L + m - m_new)
        C_new = w_prev[..., None, None] * C + jnp.einsum('bhs,bhsv,bhsk->bhvk', w_s, vj, kj)
        n_new = w_prev[..., None] * n + jnp.einsum('bhs,bhsk->bhk', w_s, kj)
        return (C_new, n_new, m_new), h

    init = (jnp.zeros((B, H, Dh, Dh), f32), jnp.zeros((B, H, Dh), f32), jnp.zeros((B, H), f32))
    _, hs = lax.scan(step, init, (qc, kc, vc, ic, fc))
    return hs.transpose(1, 0, 3, 2, 4).reshape(B, S, H * Dh).astype(v.dtype)


def token_mixer(h, w_in_l, conv_ml_l, pool_w_l, pool_scale_l, ig_bias_l, fg_bias_l,
                w_branch_l, w_out_l):
    B, S, _ = h.shape
    proj = h @ w_in_l
    (a_pool, sb_q, sb_k, sb_v, ml_qk, ml_v, ml_o, ml_i, ml_f,
     gate_logits) = jnp.split(proj, SPLIT_POINTS, axis=-1)
    y_pool = pool_mixer(a_pool, pool_w_l, pool_scale_l)
    shp_sb = (B, S, SB_HEADS, SB_HEAD_DIM)
    y_sb = stick_breaking_attention(sb_q.reshape(shp_sb), sb_k.reshape(shp_sb),
                                    sb_v.reshape(shp_sb)).reshape(B, S, D_BRANCH)
    qk = jax.nn.silu(causal_dwconv(ml_qk, conv_ml_l))
    ml_q, ml_k = jnp.split(qk, 2, axis=-1)
    shp_ml = (B, S, ML_HEADS, ML_HEAD_DIM)
    y_ml = mlstm(ml_q.reshape(shp_ml), ml_k.reshape(shp_ml), ml_v.reshape(shp_ml),
                 ml_i + ig_bias_l, ml_f + fg_bias_l)
    y_ml = y_ml * jax.nn.sigmoid(ml_o)
    gates = jax.nn.sigmoid(gate_logits).reshape(B, S, N_BRANCH, D_MODEL)
    merged = gates[:, :, 0] * (y_pool @ w_branch_l[0])
    merged = merged + gates[:, :, 1] * (y_sb @ w_branch_l[1])
    merged = merged + gates[:, :, 2] * (y_ml @ w_branch_l[2])
    return merged @ w_out_l


def conv_ffn(h, w_up_l, conv_ff_l, w_down_l):
    up = causal_dwconv(h @ w_up_l, conv_ff_l)
    val, gate = jnp.split(up, 2, axis=-1)
    return (jax.nn.silu(gate) * val) @ w_down_l


def setup_inputs(seed: int = 0) -> dict:
    key = jax.random.key(seed)
    ks = jax.random.split(key, 17)

    def nrm(k, shape, s):
        return jax.random.normal(k, shape, jnp.float32) * s

    x = nrm(ks[0], (BATCH, SEQ, D_MODEL), 1.0)
    c = nrm(ks[1], (BATCH, D_MODEL), 1.0)
    w_ada = nrm(ks[2], (DEPTH, D_MODEL, 6 * D_MODEL), 0.1 * D_MODEL ** -0.5)
    b_ada = nrm(ks[3], (DEPTH, 6 * D_MODEL), 0.02)
    w_in = nrm(ks[4], (DEPTH, D_MODEL, D_IN), D_MODEL ** -0.5)
    conv_ml = nrm(ks[5], (DEPTH, ML_CONV, 2 * D_BRANCH), ML_CONV ** -0.5)
    pool_w = nrm(ks[6], (DEPTH, POOL_GROUPS, POOL_GROUP_DIM, POOL_GROUP_DIM), POOL_GROUP_DIM ** -0.5)
    pool_scale = 1.0 + nrm(ks[7], (DEPTH, D_BRANCH), 0.02)
    ig_bias = nrm(ks[8], (DEPTH, ML_HEADS), 0.1)
    fg_bias = jnp.linspace(3.0, 6.0, ML_HEADS, dtype=jnp.float32)[None, :] + nrm(ks[9], (DEPTH, ML_HEADS), 0.1)
    w_branch = nrm(ks[10], (DEPTH, N_BRANCH, D_BRANCH, D_MODEL), D_BRANCH ** -0.5)
    w_out = nrm(ks[11], (DEPTH, D_MODEL, D_MODEL), BETA * D_MODEL ** -0.5)
    w_up = nrm(ks[12], (DEPTH, D_MODEL, 2 * D_FF), D_MODEL ** -0.5)
    conv_ff = nrm(ks[13], (DEPTH, FFN_CONV, 2 * D_FF), FFN_CONV ** -0.5)
    w_down = nrm(ks[14], (DEPTH, D_FF, D_MODEL), BETA * D_FF ** -0.5)
    ln_g = 1.0 + nrm(ks[15], (DEPTH, 2, D_MODEL), 0.02)
    ln_b = nrm(ks[16], (DEPTH, 2, D_MODEL), 0.02)
    return {"x": x, "c": c, "w_ada": w_ada, "b_ada": b_ada, "w_in": w_in,
            "conv_ml": conv_ml, "pool_w": pool_w, "pool_scale": pool_scale,
            "ig_bias": ig_bias, "fg_bias": fg_bias, "w_branch": w_branch, "w_out": w_out,
            "w_up": w_up, "conv_ff": conv_ff, "w_down": w_down, "ln_g": ln_g, "ln_b": ln_b}


def reference(x, c, w_ada, b_ada, w_in, conv_ml, pool_w, pool_scale, ig_bias, fg_bias,
              w_branch, w_out, w_up, conv_ff, w_down, ln_g, ln_b):
    c_act = jax.nn.silu(c)
    for l in range(DEPTH):
        mod = c_act @ w_ada[l] + b_ada[l]
        sh1, sc1, g1, sh2, sc2, g2 = jnp.split(mod, 6, axis=-1)
        h = x * (1.0 + sc1[:, None]) + sh1[:, None]
        y = token_mixer(h, w_in[l], conv_ml[l], pool_w[l], pool_scale[l], ig_bias[l],
                        fg_bias[l], w_branch[l], w_out[l])
        x = layer_norm(ALPHA * x + (1.0 + g1[:, None]) * y, ln_g[l, 0], ln_b[l, 0])
        h = x * (1.0 + sc2[:, None]) + sh2[:, None]
        y = conv_ffn(h, w_up[l], conv_ff[l], w_down[l])
        x = layer_norm(ALPHA * x + (1.0 + g2[:, None]) * y, ln_g[l, 1], ln_b[l, 1])
    return x
```

```python
import functools

import jax
import jax.numpy as jnp
from jax import lax
from jax.experimental import pallas as pl
from jax.experimental.pallas import tpu as pltpu

F32 = jnp.float32
BF16 = jnp.bfloat16

D_BRANCH = 1024
POOL_WINDOWS = (2, 4, 8, 16)
POOL_GROUP_DIM = 256
POOL_HALO = 16
SB_HEADS = 8
SB_HEAD_DIM = 128
ML_HEADS = 4
ML_HEAD_DIM = 256
ML_CONV = 4
FFN_CONV = 3
LN_EPS = 1e-5
GATE_COL = 8 * D_BRANCH
IF_COL = 8 * D_BRANCH
IF_PAD = 128
ROW_HALO = 16
CONV_HALO = 8

SB_LOG_ZERO = -88.0

VMEM_LIMIT_BYTES = 48 * 1024 * 1024


def _params(*sem):
    return pltpu.CompilerParams(dimension_semantics=sem, vmem_limit_bytes=VMEM_LIMIT_BYTES)


def _pick(n, cands):
    for c in cands:
        if n % c == 0:
            return c
    raise ValueError(f"no tile in {cands} divides {n}")


def _dot(a, b):
    return jnp.dot(a, b, preferred_element_type=F32)


def _dot_nt(a, b):
    return lax.dot_general(a, b, (((1,), (1,)), ((), ())), preferred_element_type=F32)


def _dot_tn(a, b):
    return lax.dot_general(a, b, (((0,), (0,)), ((), ())), preferred_element_type=F32)


def _log_sigmoid_parts(z):
    return jnp.log1p(jnp.exp(-jnp.abs(z)))


def _split_bf16(a, terms):
    parts = []
    rem = a
    for _ in range(terms):
        p = rem.astype(BF16)
        parts.append(p)
        rem = rem - p.astype(F32)
    return parts


def _mod_kernel(c_ref, w_ref, b_ref, o_ref):
    c = c_ref[...]
    ca = (c * jax.nn.sigmoid(c)).astype(BF16)
    o_ref[...] = _dot(ca, w_ref[...].astype(BF16)) + b_ref[...]


def _mod_call(c, w_ada, b_ada):
    depth, d, n = w_ada.shape
    b = c.shape[0]
    tn = _pick(n, (1024, 512, 256, 128))
    return pl.pallas_call(
        _mod_kernel,
        grid=(depth, n // tn),
        in_specs=[
            pl.BlockSpec((b, d), lambda l, j: (0, 0)),
            pl.BlockSpec((None, d, tn), lambda l, j: (l, 0, j)),
            pl.BlockSpec((None, 1, tn), lambda l, j: (l, 0, j)),
        ],
        out_specs=pl.BlockSpec((None, b, tn), lambda l, j: (l, 0, j)),
        out_shape=jax.ShapeDtypeStruct((depth, b, n), F32),
        compiler_params=_params("parallel", "parallel"),
        name="adaln_mod",
    )(c, w_ada, b_ada.reshape(depth, 1, n))


def _inproj_kernel(x_ref, sc_ref, sh_ref, w_ref, o_ref, h_ref):
    @pl.when(pl.program_id(2) == 0)
    def _():
        h_ref[...] = (x_ref[...] * (1.0 + sc_ref[...]) + sh_ref[...]).astype(h_ref.dtype)

    o_ref[...] = _dot(h_ref[...], w_ref[...]).astype(o_ref.dtype)


def _inproj_call(x, sc, sh, w, out_dtype, name):
    b, s, d = x.shape
    n = w.shape[1]
    tm = _pick(s, (1024, 512, 256))
    tn = _pick(n, (1024, 512, 256, 128))
    return pl.pallas_call(
        _inproj_kernel,
        grid=(b, s // tm, n // tn),
        in_specs=[
            pl.BlockSpec((None, tm, d), lambda i, j, k: (i, j, 0)),
            pl.BlockSpec((None, 1, d), lambda i, j, k: (i, 0, 0)),
            pl.BlockSpec((None, 1, d), lambda i, j, k: (i, 0, 0)),
            pl.BlockSpec((d, tn), lambda i, j, k: (0, k)),
        ],
        out_specs=pl.BlockSpec((None, tm, tn), lambda i, j, k: (i, j, k)),
        out_shape=jax.ShapeDtypeStruct((b, s, n), out_dtype),
        scratch_shapes=[pltpu.VMEM((tm, d), BF16)],
        compiler_params=_params("parallel", "parallel", "arbitrary"),
        name=name,
    )(x, sc, sh, w)


def _pool_kernel(a_ref, w_ref, scale_ref, o_ref, prev_ref, *, t):
    st = pl.program_id(1)

    @pl.when(st == 0)
    def _():
        prev_ref[...] = jnp.zeros_like(prev_ref)

    row = lax.broadcasted_iota(jnp.int32, (t, t), 0)
    col = lax.broadcasted_iota(jnp.int32, (t, t), 1)
    prow = lax.broadcasted_iota(jnp.int32, (t, POOL_HALO), 0)
    pcol = lax.broadcasted_iota(jnp.int32, (t, POOL_HALO), 1) - POOL_HALO
    t_abs = st * t + lax.broadcasted_iota(jnp.int32, (t, 1), 0)
    for g, win in enumerate(POOL_WINDOWS):
        cs = slice(g * POOL_GROUP_DIM, (g + 1) * POOL_GROUP_DIM)
        a = a_ref[:, cs]
        band = jnp.where((col <= row) & (col > row - win), 1.0, 0.0).astype(BF16)
        pband = jnp.where(pcol > prow - win, 1.0, 0.0).astype(BF16)
        wsum = _dot(band, a) + _dot(pband, prev_ref[:, cs])
        cnt = jnp.minimum(t_abs + 1, win).astype(F32)
        diff = (wsum / cnt - a.astype(F32)).astype(BF16)
        o_ref[:, cs] = (_dot(diff, w_ref[g]) * scale_ref[:, cs]).astype(o_ref.dtype)
    prev_ref[...] = a_ref[t - POOL_HALO:, :]


def _pool_call(main, pool_w, pool_scale):
    b, s, _ = main.shape
    t = 256
    return pl.pallas_call(
        functools.partial(_pool_kernel, t=t),
        grid=(b, s // t),
        in_specs=[
            pl.BlockSpec((None, t, D_BRANCH), lambda i, j: (i, j, 0)),
            pl.BlockSpec(pool_w.shape, lambda i, j: (0, 0, 0)),
            pl.BlockSpec((1, D_BRANCH), lambda i, j: (0, 0)),
        ],
        out_specs=pl.BlockSpec((None, t, D_BRANCH), lambda i, j: (i, j, 0)),
        out_shape=jax.ShapeDtypeStruct((b, s, D_BRANCH), BF16),
        scratch_shapes=[pltpu.VMEM((POOL_HALO, D_BRANCH), BF16)],
        compiler_params=_params("parallel", "arbitrary"),
        name="pool_mixer",
    )(main, pool_w, pool_scale)


def _sb_kernel(q_ref, k_ref, v_ref, o_ref, *, tq):
    qb = pl.program_id(1)
    scale = SB_HEAD_DIM ** -0.5
    row = lax.broadcasted_iota(jnp.int32, (tq, tq), 0)
    col = lax.broadcasted_iota(jnp.int32, (tq, tq), 1)
    below = row > col
    after_sum = jnp.where(below, 1.0, 0.0).astype(BF16)

    def tile(q, kblk, vblk, r, diag):
        z = _dot_nt(q, kblk) * scale
        l = _log_sigmoid_parts(z)
        log_beta = jnp.minimum(z, 0.0) - l
        log_keep = jnp.minimum(-z, 0.0) - l
        if diag:
            log_keep = jnp.where(below, log_keep, 0.0)
        hi, lo = _split_bf16(log_keep, 2)
        after = _dot(hi, after_sum) + _dot(lo, after_sum) + r
        attn = jnp.exp(log_beta + after)
        if diag:
            attn = jnp.where(below, attn, 0.0)
        pv = _dot(attn.astype(BF16), vblk)
        return pv, r + jnp.sum(log_keep, axis=1, keepdims=True)

    diag_rows = pl.ds(pl.multiple_of(qb * tq, tq), tq)
    for h in range(SB_HEADS):
        hs = slice(h * SB_HEAD_DIM, (h + 1) * SB_HEAD_DIM)
        q = q_ref[:, hs]
        acc0, r0 = tile(q, k_ref[diag_rows, hs], v_ref[diag_rows, hs], jnp.zeros((tq, 1), F32), True)

        def cond(carry):
            kb, _, _, rmax = carry
            return jnp.logical_and(kb >= 0, rmax > SB_LOG_ZERO)

        def body(carry, q=q, hs=hs):
            kb, acc, r, _ = carry
            rows = pl.ds(pl.multiple_of(kb * tq, tq), tq)
            pv, rn = tile(q, k_ref[rows, hs], v_ref[rows, hs], r, False)
            return kb - 1, acc + pv, rn, jnp.max(rn)

        _, acc, _, _ = lax.while_loop(cond, body, (qb - 1, acc0, r0, jnp.max(r0)))
        o_ref[:, hs] = acc.astype(o_ref.dtype)


def _sb_call(main):
    b, s, _ = main.shape
    tq = 128
    return pl.pallas_call(
        functools.partial(_sb_kernel, tq=tq),
        grid=(b, s // tq),
        in_specs=[
            pl.BlockSpec((None, tq, D_BRANCH), lambda i, j: (i, j, 1)),
            pl.BlockSpec((None, s, D_BRANCH), lambda i, j: (i, 0, 2)),
            pl.BlockSpec((None, s, D_BRANCH), lambda i, j: (i, 0, 3)),
        ],
        out_specs=pl.BlockSpec((None, tq, D_BRANCH), lambda i, j: (i, j, 0)),
        out_shape=jax.ShapeDtypeStruct((b, s, D_BRANCH), BF16),
        compiler_params=_params("parallel", "arbitrary"),
        name="stick_breaking",
    )(main, main, main)


def _mlstm_kernel(qk_ref, v_ref, og_ref, if_ref, ift_ref, cw_ref, bcol_ref, brow_ref, o_ref,
                  ext_ref, ct_ref, n_ref, m_ref, *, chunk):
    L = chunk

    @pl.when(pl.program_id(1) == 0)
    def _():
        ext_ref[0:CONV_HALO, :] = jnp.zeros((CONV_HALO, ext_ref.shape[1]), F32)
        ct_ref[...] = jnp.zeros_like(ct_ref)
        n_ref[...] = jnp.zeros_like(n_ref)
        m_ref[...] = jnp.zeros_like(m_ref)

    ext_ref[CONV_HALO:, :] = qk_ref[...].astype(F32)
    ext = ext_ref[...]
    cw = cw_ref[...]
    conv = ext * cw[ML_CONV - 1:ML_CONV, :]
    for j in range(1, ML_CONV):
        conv = conv + pltpu.roll(ext, j, axis=0) * cw[ML_CONV - 1 - j:ML_CONV - j, :]
    conv = conv[CONV_HALO:, :]
    qk = conv * jax.nn.sigmoid(conv)
    ext_ref[0:CONV_HALO, :] = ext_ref[L:L + CONV_HALO, :]

    if_c = if_ref[...] + bcol_ref[...]
    if_r = ift_ref[...] + brow_ref[...]
    logf_c = jnp.minimum(if_c, 0.0) - _log_sigmoid_parts(if_c)
    logf_r = jnp.minimum(if_r, 0.0) - _log_sigmoid_parts(if_r)
    row = lax.broadcasted_iota(jnp.int32, (L, L), 0)
    col = lax.broadcasted_iota(jnp.int32, (L, L), 1)
    causal = col <= row
    incl_lower = jnp.where(causal, 1.0, 0.0).astype(BF16)
    incl_upper = jnp.where(row <= col, 1.0, 0.0).astype(BF16)
    b_c = sum(_dot(incl_lower, p) for p in _split_bf16(logf_c, 3))
    b_r = sum(_dot(p, incl_upper) for p in _split_bf16(logf_r, 3))

    for h in range(ML_HEADS):
        hs = slice(h * ML_HEAD_DIM, (h + 1) * ML_HEAD_DIM)
        ks = slice(D_BRANCH + h * ML_HEAD_DIM, D_BRANCH + (h + 1) * ML_HEAD_DIM)
        bcol = b_c[:, ML_HEADS + h:ML_HEADS + h + 1]
        icol = if_c[:, h:h + 1]
        brow = b_r[ML_HEADS + h:ML_HEADS + h + 1, :]
        irow = if_r[h:h + 1, :]
        m_prev = m_ref[h:h + 1, 0:1]

        d = jnp.where(causal, bcol - brow + irow, -jnp.inf)
        inter = bcol + m_prev
        m_t = jnp.maximum(inter, jnp.max(d, axis=1, keepdims=True))
        w_intra = jnp.exp(d - m_t)
        w_inter = jnp.exp(inter - m_t)

        q = qk[:, hs].astype(BF16)
        k32 = qk[:, ks] * (ML_HEAD_DIM ** -0.5)
        v = v_ref[:, hs]
        ct = ct_ref[h]
        nrow = n_ref[h:h + 1, :]
        s = _dot_nt(q, k32.astype(BF16)) * w_intra
        num = w_inter * _dot(q, ct.astype(BF16)) + _dot(s.astype(BF16), v)
        den = (w_inter * jnp.sum(q.astype(F32) * nrow, axis=1, keepdims=True)
               + jnp.sum(s, axis=1, keepdims=True))
        hid = num / jnp.maximum(jnp.abs(den), jnp.exp(-m_t))
        og = og_ref[:, hs].astype(F32)
        o_ref[:, hs] = (hid * jax.nn.sigmoid(og)).astype(o_ref.dtype)

        b_last = bcol[L - 1:L, :]
        dec = b_last - bcol + icol
        m_new = jnp.maximum(b_last + m_prev, jnp.max(dec, axis=0, keepdims=True))
        w_s = jnp.exp(dec - m_new)
        w_prev = jnp.exp(b_last + m_prev - m_new)
        kw = k32 * w_s
        ct_ref[h] = w_prev * ct + _dot_tn(kw.astype(BF16), v)
        n_ref[h:h + 1, :] = w_prev * nrow + jnp.sum(kw, axis=0, keepdims=True)
        m_ref[h:h + 1, :] = jnp.broadcast_to(m_new, (1, m_ref.shape[1]))


def _mlstm_call(main, ifg, ifg_t, conv_w, bias_col, bias_row):
    b, s, _ = main.shape
    chunk = 256
    return pl.pallas_call(
        functools.partial(_mlstm_kernel, chunk=chunk),
        grid=(b, s // chunk),
        in_specs=[
            pl.BlockSpec((None, chunk, 2 * D_BRANCH), lambda i, j: (i, j, 2)),
            pl.BlockSpec((None, chunk, D_BRANCH), lambda i, j: (i, j, 6)),
            pl.BlockSpec((None, chunk, D_BRANCH), lambda i, j: (i, j, 7)),
            pl.BlockSpec((None, chunk, IF_PAD), lambda i, j: (i, j, 0)),
            pl.BlockSpec((None, 2 * ML_HEADS, chunk), lambda i, j: (i, 0, j)),
            pl.BlockSpec((ML_CONV, 2 * D_BRANCH), lambda i, j: (0, 0)),
            pl.BlockSpec((1, IF_PAD), lambda i, j: (0, 0)),
            pl.BlockSpec((2 * ML_HEADS, 1), lambda i, j: (0, 0)),
        ],
        out_specs=pl.BlockSpec((None, chunk, D_BRANCH), lambda i, j: (i, j, 0)),
        out_shape=jax.ShapeDtypeStruct((b, s, D_BRANCH), BF16),
        scratch_shapes=[
            pltpu.VMEM((chunk + CONV_HALO, 2 * D_BRANCH), F32),
            pltpu.VMEM((ML_HEADS, ML_HEAD_DIM, ML_HEAD_DIM), F32),
            pltpu.VMEM((2 * ML_HEADS, ML_HEAD_DIM), F32),
            pltpu.VMEM((2 * ML_HEADS, 128), F32),
        ],
        compiler_params=_params("parallel", "arbitrary"),
        name="mlstm",
    )(main, main, main, ifg, ifg_t, conv_w, bias_col, bias_row)


def _merge_kernel(yp_ref, ys_ref, ym_ref, g0_ref, g1_ref, g2_ref, wb_ref, o_ref):
    acc = None
    for i, (y_ref, g_ref) in enumerate(((yp_ref, g0_ref), (ys_ref, g1_ref), (ym_ref, g2_ref))):
        term = jax.nn.sigmoid(g_ref[...].astype(F32)) * _dot(y_ref[...], wb_ref[i])
        acc = term if acc is None else acc + term
    o_ref[...] = acc.astype(o_ref.dtype)


def _merge_call(y_pool, y_sb, y_ml, main, w_branch, d):
    b, s, _ = main.shape
    tm = _pick(s, (512, 256))
    tn = _pick(d, (1024, 512, 256))
    y_spec = pl.BlockSpec((None, tm, D_BRANCH), lambda i, j, k: (i, j, 0))

    def gate_spec(br):
        base = (GATE_COL + br * d) // tn
        return pl.BlockSpec((None, tm, tn), lambda i, j, k: (i, j, base + k))

    return pl.pallas_call(
        _merge_kernel,
        grid=(b, s // tm, d // tn),
        in_specs=[y_spec, y_spec, y_spec, gate_spec(0), gate_spec(1), gate_spec(2),
                  pl.BlockSpec((3, D_BRANCH, tn), lambda i, j, k: (0, 0, k))],
        out_specs=pl.BlockSpec((None, tm, tn), lambda i, j, k: (i, j, k)),
        out_shape=jax.ShapeDtypeStruct((b, s, d), BF16),
        compiler_params=_params("parallel", "parallel", "arbitrary"),
        name="branch_merge",
    )(y_pool, y_sb, y_ml, main, main, main, w_branch)


def _proj_ln_kernel(a_ref, w_ref, x_ref, g_ref, lng_ref, lnb_ref, o_ref, acc_ref, *, alpha):
    k = pl.program_id(2)

    @pl.when(k == 0)
    def _():
        acc_ref[...] = jnp.zeros_like(acc_ref)

    acc_ref[...] += _dot(a_ref[...], w_ref[...])

    @pl.when(k == pl.num_programs(2) - 1)
    def _():
        r = alpha * x_ref[...] + (1.0 + g_ref[...]) * acc_ref[...]
        mu = jnp.mean(r, axis=-1, keepdims=True)
        cen = r - mu
        var = jnp.mean(cen * cen, axis=-1, keepdims=True)
        o_ref[...] = cen * lax.rsqrt(var + LN_EPS) * lng_ref[...] + lnb_ref[...]


def _proj_ln_call(a, w, x, g, ln_g, ln_b, alpha, name):
    b, s, d = x.shape
    kdim = w.shape[0]
    tm = _pick(s, (512, 256))
    tk = _pick(kdim, (512, 256))
    return pl.pallas_call(
        functools.partial(_proj_ln_kernel, alpha=alpha),
        grid=(b, s // tm, kdim // tk),
        in_specs=[
            pl.BlockSpec((None, tm, tk), lambda i, j, k: (i, j, k)),
            pl.BlockSpec((tk, d), lambda i, j, k: (k, 0)),
            pl.BlockSpec((None, tm, d), lambda i, j, k: (i, j, 0)),
            pl.BlockSpec((None, 1, d), lambda i, j, k: (i, 0, 0)),
            pl.BlockSpec((1, d), lambda i, j, k: (0, 0)),
            pl.BlockSpec((1, d), lambda i, j, k: (0, 0)),
        ],
        out_specs=pl.BlockSpec((None, tm, d), lambda i, j, k: (i, j, 0)),
        out_shape=jax.ShapeDtypeStruct((b, s, d), F32),
        scratch_shapes=[pltpu.VMEM((tm, d), F32)],
        compiler_params=_params("parallel", "parallel", "arbitrary"),
        name=name,
    )(a, w, x, g, ln_g, ln_b)


def _ffn_up_kernel(x_ref, xh_ref, sc_ref, sh_ref, wv_ref, wg_ref, cv_ref, cg_ref, o_ref, h_ref):
    @pl.when(pl.program_id(2) == 0)
    def _():
        sc = 1.0 + sc_ref[...]
        sh = sh_ref[...]
        halo = jnp.where(pl.program_id(1) > 0, xh_ref[...] * sc + sh, 0.0)
        h_ref[0:ROW_HALO, :] = halo.astype(h_ref.dtype)
        h_ref[ROW_HALO:, :] = (x_ref[...] * sc + sh).astype(h_ref.dtype)

    h = h_ref[...]

    def conv_half(w_ref, c_ref):
        up = _dot(h, w_ref[...])
        c = c_ref[...]
        out = up * c[FFN_CONV - 1:FFN_CONV, :]
        for j in range(1, FFN_CONV):
            out = out + pltpu.roll(up, j, axis=0) * c[FFN_CONV - 1 - j:FFN_CONV - j, :]
        return out[ROW_HALO:, :]

    val = conv_half(wv_ref, cv_ref)
    gate = conv_half(wg_ref, cg_ref)
    o_ref[...] = (gate * jax.nn.sigmoid(gate) * val).astype(o_ref.dtype)


def _ffn_up_call(x, sc, sh, w_up, conv_ff):
    b, s, d = x.shape
    dff = w_up.shape[1] // 2
    tm = _pick(s, (512, 256))
    tn = _pick(dff, (512, 256))
    nff = dff // tn
    halo_blocks = tm // ROW_HALO
    return pl.pallas_call(
        _ffn_up_kernel,
        grid=(b, s // tm, nff),
        in_specs=[
            pl.BlockSpec((None, tm, d), lambda i, j, k: (i, j, 0)),
            pl.BlockSpec((None, ROW_HALO, d), lambda i, j, k: (i, jnp.maximum(j * halo_blocks - 1, 0), 0)),
            pl.BlockSpec((None, 1, d), lambda i, j, k: (i, 0, 0)),
            pl.BlockSpec((None, 1, d), lambda i, j, k: (i, 0, 0)),
            pl.BlockSpec((d, tn), lambda i, j, k: (0, k)),
            pl.BlockSpec((d, tn), lambda i, j, k: (0, nff + k)),
            pl.BlockSpec((FFN_CONV, tn), lambda i, j, k: (0, k)),
            pl.BlockSpec((FFN_CONV, tn), lambda i, j, k: (0, nff + k)),
        ],
        out_specs=pl.BlockSpec((None, tm, tn), lambda i, j, k: (i, j, k)),
        out_shape=jax.ShapeDtypeStruct((b, s, dff), BF16),
        scratch_shapes=[pltpu.VMEM((tm + ROW_HALO, d), BF16)],
        compiler_params=_params("parallel", "parallel", "arbitrary"),
        name="ffn_up_conv_gate",
    )(x, x, sc, sh, w_up, w_up, conv_ff, conv_ff)


def kernel(x, c, w_ada, b_ada, w_in, conv_ml, pool_w, pool_scale, ig_bias, fg_bias, w_branch, w_out,
           w_up, conv_ff, w_down, ln_g, ln_b):
    b, s, d = x.shape
    depth = w_in.shape[0]
    alpha = (2.0 * depth) ** 0.25

    mod = _mod_call(c, w_ada, b_ada)
    for l in range(depth):
        sh1, sc1, g1, sh2, sc2, g2 = (mod[l][:, None, i * d:(i + 1) * d] for i in range(6))

        w_main = jnp.concatenate([w_in[l][:, :IF_COL], w_in[l][:, IF_COL + 2 * ML_HEADS:]], axis=1).astype(BF16)
        w_if = jnp.pad(w_in[l][:, IF_COL:IF_COL + 2 * ML_HEADS], ((0, 0), (0, IF_PAD - 2 * ML_HEADS))).astype(BF16)
        main = _inproj_call(x, sc1, sh1, w_main, BF16, "in_proj")
        ifg = _inproj_call(x, sc1, sh1, w_if, F32, "in_proj_gates")
        ifg_t = jnp.swapaxes(ifg[:, :, :2 * ML_HEADS], 1, 2)
        gate_bias = jnp.concatenate([ig_bias[l], fg_bias[l]])
        bias_col = jnp.pad(gate_bias, (0, IF_PAD - 2 * ML_HEADS)).reshape(1, IF_PAD)
        bias_row = gate_bias.reshape(2 * ML_HEADS, 1)

        y_pool = _pool_call(main, pool_w[l].astype(BF16), pool_scale[l].reshape(1, D_BRANCH))
        y_sb = _sb_call(main)
        y_ml = _mlstm_call(main, ifg, ifg_t, conv_ml[l], bias_col, bias_row)
        merged = _merge_call(y_pool, y_sb, y_ml, main, w_branch[l].astype(BF16), d)
        x = _proj_ln_call(merged, w_out[l].astype(BF16), x, g1, ln_g[l, 0].reshape(1, d),
                          ln_b[l, 0].reshape(1, d), alpha, "out_proj_ln")

        act = _ffn_up_call(x, sc2, sh2, w_up[l].astype(BF16), conv_ff[l])
        x = _proj_ln_call(act, w_down[l].astype(BF16), x, g2, ln_g[l, 1].reshape(1, d),
                          ln_b[l, 1].reshape(1, d), alpha, "ffn_down_ln")
    return x
```

```python
import functools

import jax
import jax.numpy as jnp
from jax import lax
from jax.experimental import pallas as pl
from jax.experimental.pallas import tpu as pltpu

F32 = jnp.float32
BF16 = jnp.bfloat16

D_BRANCH = 1024
POOL_WINDOWS = (2, 4, 8, 16)
POOL_GROUP_DIM = 256
POOL_HALO = 16
SB_HEADS = 8
SB_HEAD_DIM = 128
ML_HEADS = 4
ML_HEAD_DIM = 256
ML_CONV = 4
FFN_CONV = 3
FFN_SUB = 256
LN_EPS = 1e-5
GATE_COL = 8 * D_BRANCH
IF_COL = 8 * D_BRANCH
IF_PAD = 128
ROW_HALO = 16
CONV_HALO = 8

SB_LOG_ZERO = -88.0

VMEM_LIMIT_BYTES = 48 * 1024 * 1024


def _params(*sem):
    return pltpu.CompilerParams(dimension_semantics=sem, vmem_limit_bytes=VMEM_LIMIT_BYTES)


def _pick(n, cands):
    for c in cands:
        if n % c == 0:
            return c
    raise ValueError(f"no tile in {cands} divides {n}")


def _dot(a, b):
    return jnp.dot(a, b, preferred_element_type=F32)


def _dot_nt(a, b):
    return lax.dot_general(a, b, (((1,), (1,)), ((), ())), preferred_element_type=F32)


def _dot_tn(a, b):
    return lax.dot_general(a, b, (((0,), (0,)), ((), ())), preferred_element_type=F32)


def _log_sigmoid_parts(z):
    return jnp.log1p(jnp.exp(-jnp.abs(z)))


def _split_bf16(a, terms):
    parts = []
    rem = a
    for _ in range(terms):
        p = rem.astype(BF16)
        parts.append(p)
        rem = rem - p.astype(F32)
    return parts


def _mod_kernel(c_ref, w_ref, b_ref, o_ref):
    c = c_ref[...]
    ca = (c * jax.nn.sigmoid(c)).astype(BF16)
    o_ref[...] = _dot(ca, w_ref[...].astype(BF16)) + b_ref[...]


def _mod_call(c, w_ada, b_ada):
    depth, d, n = w_ada.shape
    b = c.shape[0]
    tn = _pick(n, (1024, 512, 256, 128))
    return pl.pallas_call(
        _mod_kernel,
        grid=(depth, n // tn),
        in_specs=[
            pl.BlockSpec((b, d), lambda l, j: (0, 0)),
            pl.BlockSpec((None, d, tn), lambda l, j: (l, 0, j)),
            pl.BlockSpec((None, 1, tn), lambda l, j: (l, 0, j)),
        ],
        out_specs=pl.BlockSpec((None, b, tn), lambda l, j: (l, 0, j)),
        out_shape=jax.ShapeDtypeStruct((depth, b, n), F32),
        compiler_params=_params("parallel", "parallel"),
        name="adaln_mod",
    )(c, w_ada, b_ada.reshape(depth, 1, n))


def _inproj_kernel(x_ref, sc_ref, sh_ref, w_ref, o_ref, h_ref):
    @pl.when(pl.program_id(2) == 0)
    def _():
        h_ref[...] = (x_ref[...] * (1.0 + sc_ref[...]) + sh_ref[...]).astype(h_ref.dtype)

    o_ref[...] = _dot(h_ref[...], w_ref[...]).astype(o_ref.dtype)


def _inproj_call(x, sc, sh, w, out_dtype, name):
    b, s, d = x.shape
    n = w.shape[1]
    tm = _pick(s, (1024, 512, 256))
    tn = _pick(n, (1024, 512, 256, 128))
    return pl.pallas_call(
        _inproj_kernel,
        grid=(b, s // tm, n // tn),
        in_specs=[
            pl.BlockSpec((None, tm, d), lambda i, j, k: (i, j, 0)),
            pl.BlockSpec((None, 1, d), lambda i, j, k: (i, 0, 0)),
            pl.BlockSpec((None, 1, d), lambda i, j, k: (i, 0, 0)),
            pl.BlockSpec((d, tn), lambda i, j, k: (0, k)),
        ],
        out_specs=pl.BlockSpec((None, tm, tn), lambda i, j, k: (i, j, k)),
        out_shape=jax.ShapeDtypeStruct((b, s, n), out_dtype),
        scratch_shapes=[pltpu.VMEM((tm, d), BF16)],
        compiler_params=_params("parallel", "parallel", "arbitrary"),
        name=name,
    )(x, sc, sh, w)


def _pool_kernel(a_ref, w_ref, scale_ref, o_ref, prev_ref, *, t):
    st = pl.program_id(1)

    @pl.when(st == 0)
    def _():
        prev_ref[...] = jnp.zeros_like(prev_ref)

    row = lax.broadcasted_iota(jnp.int32, (t, t), 0)
    col = lax.broadcasted_iota(jnp.int32, (t, t), 1)
    prow = lax.broadcasted_iota(jnp.int32, (t, POOL_HALO), 0)
    pcol = lax.broadcasted_iota(jnp.int32, (t, POOL_HALO), 1) - POOL_HALO
    t_abs = st * t + lax.broadcasted_iota(jnp.int32, (t, 1), 0)
    for g, win in enumerate(POOL_WINDOWS):
        cs = slice(g * POOL_GROUP_DIM, (g + 1) * POOL_GROUP_DIM)
        a = a_ref[:, cs]
        band = jnp.where((col <= row) & (col > row - win), 1.0, 0.0).astype(BF16)
        pband = jnp.where(pcol > prow - win, 1.0, 0.0).astype(BF16)
        wsum = _dot(band, a) + _dot(pband, prev_ref[:, cs])
        cnt = jnp.minimum(t_abs + 1, win).astype(F32)
        diff = (wsum / cnt - a.astype(F32)).astype(BF16)
        o_ref[:, cs] = (_dot(diff, w_ref[g]) * scale_ref[:, cs]).astype(o_ref.dtype)
    prev_ref[...] = a_ref[t - POOL_HALO:, :]


def _pool_call(main, pool_w, pool_scale):
    b, s, _ = main.shape
    t = 256
    return pl.pallas_call(
        functools.partial(_pool_kernel, t=t),
        grid=(b, s // t),
        in_specs=[
            pl.BlockSpec((None, t, D_BRANCH), lambda i, j: (i, j, 0)),
            pl.BlockSpec(pool_w.shape, lambda i, j: (0, 0, 0)),
            pl.BlockSpec((1, D_BRANCH), lambda i, j: (0, 0)),
        ],
        out_specs=pl.BlockSpec((None, t, D_BRANCH), lambda i, j: (i, j, 0)),
        out_shape=jax.ShapeDtypeStruct((b, s, D_BRANCH), BF16),
        scratch_shapes=[pltpu.VMEM((POOL_HALO, D_BRANCH), BF16)],
        compiler_params=_params("parallel", "arbitrary"),
        name="pool_mixer",
    )(main, pool_w, pool_scale)


def _sb_kernel(q_ref, k_ref, v_ref, o_ref, *, tq):
    qb = pl.program_id(1)
    scale = SB_HEAD_DIM ** -0.5
    row = lax.broadcasted_iota(jnp.int32, (tq, tq), 0)
    col = lax.broadcasted_iota(jnp.int32, (tq, tq), 1)
    below = row > col
    after_sum = jnp.where(below, 1.0, 0.0).astype(BF16)

    def sweep(rows, accs, rs):
        diag = accs is None
        heads = [slice(h * SB_HEAD_DIM, (h + 1) * SB_HEAD_DIM) for h in range(SB_HEADS)]
        zs = [_dot_nt(q_ref[:, hs], k_ref[rows, hs]) * scale for hs in heads]
        log_betas, log_keeps, his, los = [], [], [], []
        for z in zs:
            l = _log_sigmoid_parts(z)
            log_keep = jnp.minimum(-z, 0.0) - l
            if diag:
                log_keep = jnp.where(below, log_keep, 0.0)
            hi, lo = _split_bf16(log_keep, 2)
            log_betas.append(jnp.minimum(z, 0.0) - l)
            log_keeps.append(log_keep)
            his.append(hi)
            los.append(lo)
        sums = _dot(jnp.concatenate(his + los, axis=0), after_sum)
        attns, new_rs, rmax = [], [], None
        for h in range(SB_HEADS):
            after = sums[h * tq:(h + 1) * tq] + sums[(SB_HEADS + h) * tq:(SB_HEADS + h + 1) * tq]
            rowsum = jnp.sum(log_keeps[h], axis=1, keepdims=True)
            if diag:
                attn = jnp.where(below, jnp.exp(log_betas[h] + after), 0.0)
                rn = rowsum
            else:
                attn = jnp.exp(log_betas[h] + (after + rs[h]))
                rn = rs[h] + rowsum
            attns.append(attn.astype(BF16))
            new_rs.append(rn)
            rmax = rn if rmax is None else jnp.maximum(rmax, rn)
        pvs = [_dot(attns[h], v_ref[rows, hs]) for h, hs in enumerate(heads)]
        new_accs = pvs if diag else [a + p for a, p in zip(accs, pvs)]
        return tuple(new_accs), tuple(new_rs), jnp.max(rmax)

    accs0, rs0, rmax0 = sweep(pl.ds(pl.multiple_of(qb * tq, tq), tq), None, None)

    def cond(carry):
        kb, rmax, _, _ = carry
        return jnp.logical_and(kb >= 0, rmax > SB_LOG_ZERO)

    def body(carry):
        kb, _, accs, rs = carry
        accs, rs, rmax = sweep(pl.ds(pl.multiple_of(kb * tq, tq), tq), accs, rs)
        return kb - 1, rmax, accs, rs

    _, _, accs, _ = lax.while_loop(cond, body, (qb - 1, rmax0, accs0, rs0))
    for h in range(SB_HEADS):
        o_ref[:, h * SB_HEAD_DIM:(h + 1) * SB_HEAD_DIM] = accs[h].astype(o_ref.dtype)


def _sb_call(main):
    b, s, _ = main.shape
    tq = 128
    return pl.pallas_call(
        functools.partial(_sb_kernel, tq=tq),
        grid=(b, s // tq),
        in_specs=[
            pl.BlockSpec((None, tq, D_BRANCH), lambda i, j: (i, j, 1)),
            pl.BlockSpec((None, s, D_BRANCH), lambda i, j: (i, 0, 2)),
            pl.BlockSpec((None, s, D_BRANCH), lambda i, j: (i, 0, 3)),
        ],
        out_specs=pl.BlockSpec((None, tq, D_BRANCH), lambda i, j: (i, j, 0)),
        out_shape=jax.ShapeDtypeStruct((b, s, D_BRANCH), BF16),
        compiler_params=_params("parallel", "arbitrary"),
        name="stick_breaking",
    )(main, main, main)


def _mlstm_kernel(qk_ref, v_ref, og_ref, if_ref, ift_ref, cw_ref, bcol_ref, brow_ref, o_ref,
                  ext_ref, ct_ref, n_ref, m_ref, *, chunk):
    L = chunk

    @pl.when(pl.program_id(1) == 0)
    def _():
        ext_ref[0:CONV_HALO, :] = jnp.zeros((CONV_HALO, ext_ref.shape[1]), F32)
        ct_ref[...] = jnp.zeros_like(ct_ref)
        n_ref[...] = jnp.zeros_like(n_ref)
        m_ref[...] = jnp.zeros_like(m_ref)

    ext_ref[CONV_HALO:, :] = qk_ref[...].astype(F32)
    ext = ext_ref[...]
    cw = cw_ref[...]
    conv = ext * cw[ML_CONV - 1:ML_CONV, :]
    for j in range(1, ML_CONV):
        conv = conv + pltpu.roll(ext, j, axis=0) * cw[ML_CONV - 1 - j:ML_CONV - j, :]
    conv = conv[CONV_HALO:, :]
    qk = conv * jax.nn.sigmoid(conv)
    ext_ref[0:CONV_HALO, :] = ext_ref[L:L + CONV_HALO, :]

    if_c = if_ref[...] + bcol_ref[...]
    if_r = ift_ref[...] + brow_ref[...]
    logf_c = jnp.minimum(if_c, 0.0) - _log_sigmoid_parts(if_c)
    logf_r = jnp.minimum(if_r, 0.0) - _log_sigmoid_parts(if_r)
    row = lax.broadcasted_iota(jnp.int32, (L, L), 0)
    col = lax.broadcasted_iota(jnp.int32, (L, L), 1)
    causal = col <= row
    incl_lower = jnp.where(causal, 1.0, 0.0).astype(BF16)
    incl_upper = jnp.where(row <= col, 1.0, 0.0).astype(BF16)
    b_c = sum(_dot(incl_lower, p) for p in _split_bf16(logf_c, 3))
    b_r = sum(_dot(p, incl_upper) for p in _split_bf16(logf_r, 3))

    for h in range(ML_HEADS):
        hs = slice(h * ML_HEAD_DIM, (h + 1) * ML_HEAD_DIM)
        ks = slice(D_BRANCH + h * ML_HEAD_DIM, D_BRANCH + (h + 1) * ML_HEAD_DIM)
        bcol = b_c[:, ML_HEADS + h:ML_HEADS + h + 1]
        icol = if_c[:, h:h + 1]
        brow = b_r[ML_HEADS + h:ML_HEADS + h + 1, :]
        irow = if_r[h:h + 1, :]
        m_prev = m_ref[h:h + 1, 0:1]

        d = jnp.where(causal, bcol - brow + irow, -jnp.inf)
        inter = bcol + m_prev
        m_t = jnp.maximum(inter, jnp.max(d, axis=1, keepdims=True))
        w_intra = jnp.exp(d - m_t)
        w_inter = jnp.exp(inter - m_t)

        q = qk[:, hs].astype(BF16)
        k32 = qk[:, ks] * (ML_HEAD_DIM ** -0.5)
        v = v_ref[:, hs]
        ct = ct_ref[h]
        nrow = n_ref[h:h + 1, :]
        s = _dot_nt(q, k32.astype(BF16)) * w_intra
        num = w_inter * _dot(q, ct.astype(BF16)) + _dot(s.astype(BF16), v)
        den = (w_inter * jnp.sum(q.astype(F32) * nrow, axis=1, keepdims=True)
               + jnp.sum(s, axis=1, keepdims=True))
        hid = num / jnp.maximum(jnp.abs(den), jnp.exp(-m_t))
        og = og_ref[:, hs].astype(F32)
        o_ref[:, hs] = (hid * jax.nn.sigmoid(og)).astype(o_ref.dtype)

        b_last = bcol[L - 1:L, :]
        dec = b_last - bcol + icol
        m_new = jnp.maximum(b_last + m_prev, jnp.max(dec, axis=0, keepdims=True))
        w_s = jnp.exp(dec - m_new)
        w_prev = jnp.exp(b_last + m_prev - m_new)
        kw = k32 * w_s
        ct_ref[h] = w_prev * ct + _dot_tn(kw.astype(BF16), v)
        n_ref[h:h + 1, :] = w_prev * nrow + jnp.sum(kw, axis=0, keepdims=True)
        m_ref[h:h + 1, :] = jnp.broadcast_to(m_new, (1, m_ref.shape[1]))


def _mlstm_call(main, ifg, ifg_t, conv_w, bias_col, bias_row):
    b, s, _ = main.shape
    chunk = 256
    return pl.pallas_call(
        functools.partial(_mlstm_kernel, chunk=chunk),
        grid=(b, s // chunk),
        in_specs=[
            pl.BlockSpec((None, chunk, 2 * D_BRANCH), lambda i, j: (i, j, 2)),
            pl.BlockSpec((None, chunk, D_BRANCH), lambda i, j: (i, j, 6)),
            pl.BlockSpec((None, chunk, D_BRANCH), lambda i, j: (i, j, 7)),
            pl.BlockSpec((None, chunk, IF_PAD), lambda i, j: (i, j, 0)),
            pl.BlockSpec((None, 2 * ML_HEADS, chunk), lambda i, j: (i, 0, j)),
            pl.BlockSpec((ML_CONV, 2 * D_BRANCH), lambda i, j: (0, 0)),
            pl.BlockSpec((1, IF_PAD), lambda i, j: (0, 0)),
            pl.BlockSpec((2 * ML_HEADS, 1), lambda i, j: (0, 0)),
        ],
        out_specs=pl.BlockSpec((None, chunk, D_BRANCH), lambda i, j: (i, j, 0)),
        out_shape=jax.ShapeDtypeStruct((b, s, D_BRANCH), BF16),
        scratch_shapes=[
            pltpu.VMEM((chunk + CONV_HALO, 2 * D_BRANCH), F32),
            pltpu.VMEM((ML_HEADS, ML_HEAD_DIM, ML_HEAD_DIM), F32),
            pltpu.VMEM((2 * ML_HEADS, ML_HEAD_DIM), F32),
            pltpu.VMEM((2 * ML_HEADS, 128), F32),
        ],
        compiler_params=_params("parallel", "arbitrary"),
        name="mlstm",
    )(main, main, main, ifg, ifg_t, conv_w, bias_col, bias_row)


def _residual_ln(x, y, g, ln_g, ln_b, alpha):
    r = alpha * x + (1.0 + g) * y
    mu = jnp.mean(r, axis=-1, keepdims=True)
    cen = r - mu
    var = jnp.mean(cen * cen, axis=-1, keepdims=True)
    return cen * lax.rsqrt(var + LN_EPS) * ln_g + ln_b


def _mixer_out_kernel(yp_ref, ys_ref, ym_ref, g0_ref, g1_ref, g2_ref, wb_ref, wo_ref, x_ref, g_ref,
                      lng_ref, lnb_ref, o_ref, *, alpha):
    acc = None
    for i, (y_ref, gl_ref) in enumerate(((yp_ref, g0_ref), (ys_ref, g1_ref), (ym_ref, g2_ref))):
        term = jax.nn.sigmoid(gl_ref[...].astype(F32)) * _dot(y_ref[...], wb_ref[i])
        acc = term if acc is None else acc + term
    y = _dot(acc.astype(BF16), wo_ref[...])
    o_ref[...] = _residual_ln(x_ref[...], y, g_ref[...], lng_ref[...], lnb_ref[...], alpha)


def _resident(shape):
    return pl.BlockSpec(shape, lambda *_: (0,) * len(shape), pipeline_mode=pl.Buffered(1))


def _mixer_out_call(y_pool, y_sb, y_ml, main, w_branch, w_out, x, g, ln_g, ln_b, alpha):
    b, s, d = x.shape
    tm = 256
    y_spec = pl.BlockSpec((None, tm, D_BRANCH), lambda i, j: (i, j, 0))

    def gate_spec(br):
        blk = GATE_COL // d + br
        return pl.BlockSpec((None, tm, d), lambda i, j: (i, j, blk))

    return pl.pallas_call(
        functools.partial(_mixer_out_kernel, alpha=alpha),
        grid=(b, s // tm),
        in_specs=[y_spec, y_spec, y_spec, gate_spec(0), gate_spec(1), gate_spec(2),
                  _resident(w_branch.shape), _resident(w_out.shape),
                  pl.BlockSpec((None, tm, d), lambda i, j: (i, j, 0)),
                  pl.BlockSpec((None, 1, d), lambda i, j: (i, 0, 0)),
                  pl.BlockSpec((1, d), lambda i, j: (0, 0)),
                  pl.BlockSpec((1, d), lambda i, j: (0, 0))],
        out_specs=pl.BlockSpec((None, tm, d), lambda i, j: (i, j, 0)),
        out_shape=jax.ShapeDtypeStruct((b, s, d), F32),
        compiler_params=_params("parallel", "parallel"),
        name="mixer_out_ln",
    )(y_pool, y_sb, y_ml, main, main, main, w_branch, w_out, x, g, ln_g, ln_b)


def _ffn_kernel(x_ref, xh_ref, sc_ref, sh_ref, wv_ref, wg_ref, cv_ref, cg_ref, wd_ref, g_ref,
                lng_ref, lnb_ref, o_ref, h_ref, *, alpha):
    n = pl.program_id(2)

    @pl.when(n == 0)
    def _():
        sc = 1.0 + sc_ref[...]
        sh = sh_ref[...]
        halo = jnp.where(pl.program_id(1) > 0, xh_ref[...] * sc + sh, 0.0)
        h_ref[0:ROW_HALO, :] = halo.astype(h_ref.dtype)
        h_ref[ROW_HALO:, :] = (x_ref[...] * sc + sh).astype(h_ref.dtype)
        o_ref[...] = jnp.zeros_like(o_ref)

    h = h_ref[...]

    def conv(up, c_ref, cs):
        c = c_ref[:, cs]
        out = up * c[FFN_CONV - 1:FFN_CONV, :]
        for j in range(1, FFN_CONV):
            out = out + pltpu.roll(up, j, axis=0) * c[FFN_CONV - 1 - j:FFN_CONV - j, :]
        return out[ROW_HALO:, :]

    subs = [slice(j * FFN_SUB, (j + 1) * FFN_SUB) for j in range(wv_ref.shape[1] // FFN_SUB)]
    ups = [(_dot(h, wv_ref[:, cs]), _dot(h, wg_ref[:, cs])) for cs in subs]
    part = None
    for cs, (up_val, up_gate) in zip(subs, ups):
        val = conv(up_val, cv_ref, cs)
        gate = conv(up_gate, cg_ref, cs)
        act = (gate * jax.nn.sigmoid(gate) * val).astype(BF16)
        down = _dot(act, wd_ref[cs, :])
        part = down if part is None else part + down
    o_ref[...] += part

    @pl.when(n == pl.num_programs(2) - 1)
    def _():
        o_ref[...] = _residual_ln(x_ref[...], o_ref[...], g_ref[...], lng_ref[...], lnb_ref[...], alpha)


def _ffn_call(x, sc, sh, w_up, conv_ff, w_down, g, ln_g, ln_b, alpha):
    b, s, d = x.shape
    dff = w_down.shape[0]
    tm = _pick(s, (512, 256))
    tn = _pick(dff, (512, 256))
    nff = dff // tn
    halo_blocks = tm // ROW_HALO
    return pl.pallas_call(
        functools.partial(_ffn_kernel, alpha=alpha),
        grid=(b, s // tm, nff),
        in_specs=[
            pl.BlockSpec((None, tm, d), lambda i, j, k: (i, j, 0)),
            pl.BlockSpec((None, ROW_HALO, d), lambda i, j, k: (i, jnp.maximum(j * halo_blocks - 1, 0), 0)),
            pl.BlockSpec((None, 1, d), lambda i, j, k: (i, 0, 0)),
            pl.BlockSpec((None, 1, d), lambda i, j, k: (i, 0, 0)),
            pl.BlockSpec((d, tn), lambda i, j, k: (0, k)),
            pl.BlockSpec((d, tn), lambda i, j, k: (0, nff + k)),
            pl.BlockSpec((FFN_CONV, tn), lambda i, j, k: (0, k)),
            pl.BlockSpec((FFN_CONV, tn), lambda i, j, k: (0, nff + k)),
            pl.BlockSpec((tn, d), lambda i, j, k: (k, 0)),
            pl.BlockSpec((None, 1, d), lambda i, j, k: (i, 0, 0)),
            pl.BlockSpec((1, d), lambda i, j, k: (0, 0)),
            pl.BlockSpec((1, d), lambda i, j, k: (0, 0)),
        ],
        out_specs=pl.BlockSpec((None, tm, d), lambda i, j, k: (i, j, 0)),
        out_shape=jax.ShapeDtypeStruct((b, s, d), F32),
        scratch_shapes=[pltpu.VMEM((tm + ROW_HALO, d), BF16)],
        compiler_params=_params("parallel", "parallel", "arbitrary"),
        name="ffn_conv_gate_down_ln",
    )(x, x, sc, sh, w_up, w_up, conv_ff, conv_ff, w_down, g, ln_g, ln_b)


def kernel(x, c, w_ada, b_ada, w_in, conv_ml, pool_w, pool_scale, ig_bias, fg_bias, w_branch, w_out,
           w_up, conv_ff, w_down, ln_g, ln_b):
    b, s, d = x.shape
    depth = w_in.shape[0]
    alpha = (2.0 * depth) ** 0.25

    mod = _mod_call(c, w_ada, b_ada)
    for l in range(depth):
        sh1, sc1, g1, sh2, sc2, g2 = (mod[l][:, None, i * d:(i + 1) * d] for i in range(6))

        w_main = jnp.concatenate([w_in[l][:, :IF_COL], w_in[l][:, IF_COL + 2 * ML_HEADS:]], axis=1).astype(BF16)
        w_if = jnp.pad(w_in[l][:, IF_COL:IF_COL + 2 * ML_HEADS], ((0, 0), (0, IF_PAD - 2 * ML_HEADS))).astype(BF16)
        main = _inproj_call(x, sc1, sh1, w_main, BF16, "in_proj")
        ifg = _inproj_call(x, sc1, sh1, w_if, F32, "in_proj_gates")
        ifg_t = jnp.swapaxes(ifg[:, :, :2 * ML_HEADS], 1, 2)
        gate_bias = jnp.concatenate([ig_bias[l], fg_bias[l]])
        bias_col = jnp.pad(gate_bias, (0, IF_PAD - 2 * ML_HEADS)).reshape(1, IF_PAD)
        bias_row = gate_bias.reshape(2 * ML_HEADS, 1)

        y_pool = _pool_call(main, pool_w[l].astype(BF16), pool_scale[l].reshape(1, D_BRANCH))
        y_sb = _sb_call(main)
        y_ml = _mlstm_call(main, ifg, ifg_t, conv_ml[l], bias_col, bias_row)
        x = _mixer_out_call(y_pool, y_sb, y_ml, main, w_branch[l].astype(BF16), w_out[l].astype(BF16), x, g1,
                            ln_g[l, 0].reshape(1, d), ln_b[l, 0].reshape(1, d), alpha)
        x = _ffn_call(x, sc2, sh2, w_up[l].astype(BF16), conv_ff[l], w_down[l].astype(BF16), g2,
                      ln_g[l, 1].reshape(1, d), ln_b[l, 1].reshape(1, d), alpha)
    return x
```

```python
import functools

import jax
import jax.numpy as jnp
from jax import lax
from jax.experimental import pallas as pl
from jax.experimental.pallas import tpu as pltpu

F32 = jnp.float32
BF16 = jnp.bfloat16

D_BRANCH = 1024
POOL_WINDOWS = (2, 4, 8, 16)
POOL_GROUP_DIM = 256
POOL_HALO = 16
SB_HEADS = 8
SB_HEAD_DIM = 128
ML_HEADS = 4
ML_HEAD_DIM = 256
ML_CONV = 4
FFN_CONV = 3
FFN_PARTS = 2
LN_EPS = 1e-5
IF_COL = 8 * D_BRANCH
IF_PAD = 128
ROW_HALO = 16
CONV_HALO = 8

LOG2_E = 1.4426950408889634

SB_ZERO_BITS = 127.0

VMEM_LIMIT_BYTES = 48 * 1024 * 1024
INPROJ_VMEM_LIMIT_BYTES = 54 * 1024 * 1024


def _params(*sem, vmem_limit_bytes=VMEM_LIMIT_BYTES):
    return pltpu.CompilerParams(dimension_semantics=sem, vmem_limit_bytes=vmem_limit_bytes)


def _pick(n, cands):
    for c in cands:
        if n % c == 0:
            return c
    raise ValueError(f"no tile in {cands} divides {n}")


def _dot(a, b):
    return jnp.dot(a, b, preferred_element_type=F32)


def _dot_nt(a, b):
    return lax.dot_general(a, b, (((1,), (1,)), ((), ())), preferred_element_type=F32)


def _dot_tn(a, b):
    return lax.dot_general(a, b, (((0,), (0,)), ((), ())), preferred_element_type=F32)


def _log_sigmoid_parts(z):
    return jnp.log1p(jnp.exp(-jnp.abs(z)))


def _split_bf16(a, terms):
    parts = []
    rem = a
    for _ in range(terms):
        p = rem.astype(BF16)
        parts.append(p)
        rem = rem - p.astype(F32)
    return parts


def _mod_kernel(c_ref, w_ref, b_ref, o_ref):
    c = c_ref[...]
    ca = (c * jax.nn.sigmoid(c)).astype(BF16)
    o_ref[...] = _dot(ca, w_ref[...].astype(BF16)) + b_ref[...]


def _mod_call(c, w_ada, b_ada):
    depth, d, n = w_ada.shape
    b = c.shape[0]
    tn = _pick(n, (1024, 512, 256, 128))
    return pl.pallas_call(
        _mod_kernel,
        grid=(depth, n // tn),
        in_specs=[
            pl.BlockSpec((b, d), lambda l, j: (0, 0)),
            pl.BlockSpec((None, d, tn), lambda l, j: (l, 0, j)),
            pl.BlockSpec((None, 1, tn), lambda l, j: (l, 0, j)),
        ],
        out_specs=pl.BlockSpec((None, b, tn), lambda l, j: (l, 0, j)),
        out_shape=jax.ShapeDtypeStruct((depth, b, n), F32),
        compiler_params=_params("parallel", "parallel"),
        name="adaln_mod",
    )(c, w_ada, b_ada.reshape(depth, 1, n))


def _inproj_kernel(x_ref, sc_ref, sh_ref, wa_ref, wb_ref, wif_ref, oa_ref, ob_ref, oif_ref, h_ref, *, na, nb):
    k = pl.program_id(2)

    @pl.when(k == 0)
    def _():
        h_ref[...] = (x_ref[...] * (1.0 + sc_ref[...]) + sh_ref[...]).astype(h_ref.dtype)

    @pl.when(k < na)
    def _():
        oa_ref[...] = _dot(h_ref[...], wa_ref[...]).astype(oa_ref.dtype)

    @pl.when(jnp.logical_and(k >= na, k < na + nb))
    def _():
        ob_ref[...] = _dot(h_ref[...], wb_ref[...]).astype(ob_ref.dtype)

    @pl.when(k == na + nb)
    def _():
        oif_ref[...] = _dot(h_ref[...], wif_ref[...])


def _inproj_call(x, sc, sh, w_a, w_b, w_if):
    b, s, d = x.shape
    tm = _pick(s, (1024, 512, 256))
    tn = _pick(w_b.shape[1], (1024, 512, 256))
    na, nb = w_a.shape[1] // tn, w_b.shape[1] // tn
    return pl.pallas_call(
        functools.partial(_inproj_kernel, na=na, nb=nb),
        grid=(b, s // tm, na + nb + 1),
        in_specs=[
            pl.BlockSpec((None, tm, d), lambda i, j, k: (i, j, 0)),
            pl.BlockSpec((None, 1, d), lambda i, j, k: (i, 0, 0)),
            pl.BlockSpec((None, 1, d), lambda i, j, k: (i, 0, 0)),
            pl.BlockSpec((d, tn), lambda i, j, k: (0, jnp.minimum(k, na - 1))),
            pl.BlockSpec((d, tn), lambda i, j, k: (0, jnp.clip(k - na, 0, nb - 1))),
            pl.BlockSpec((d, IF_PAD), lambda i, j, k: (0, 0)),
        ],
        out_specs=[
            pl.BlockSpec((None, tm, tn), lambda i, j, k: (i, j, jnp.minimum(k, na - 1))),
            pl.BlockSpec((None, tm, tn), lambda i, j, k: (i, j, jnp.clip(k - na, 0, nb - 1))),
            pl.BlockSpec((None, tm, IF_PAD), lambda i, j, k: (i, j, 0)),
        ],
        out_shape=[
            jax.ShapeDtypeStruct((b, s, w_a.shape[1]), BF16),
            jax.ShapeDtypeStruct((b, s, w_b.shape[1]), BF16),
            jax.ShapeDtypeStruct((b, s, IF_PAD), F32),
        ],
        scratch_shapes=[pltpu.VMEM((tm, d), BF16)],
        compiler_params=_params("parallel", "parallel", "arbitrary", vmem_limit_bytes=INPROJ_VMEM_LIMIT_BYTES),
        name="in_proj",
    )(x, sc, sh, w_a, w_b, w_if)


def _pool_kernel(a_ref, w_ref, scale_ref, o_ref, prev_ref, *, t):
    st = pl.program_id(1)

    @pl.when(st == 0)
    def _():
        prev_ref[...] = jnp.zeros_like(prev_ref)

    row = lax.broadcasted_iota(jnp.int32, (t, t), 0)
    col = lax.broadcasted_iota(jnp.int32, (t, t), 1)
    prow = lax.broadcasted_iota(jnp.int32, (t, POOL_HALO), 0)
    pcol = lax.broadcasted_iota(jnp.int32, (t, POOL_HALO), 1) - POOL_HALO
    t_abs = st * t + lax.broadcasted_iota(jnp.int32, (t, 1), 0)
    for g, win in enumerate(POOL_WINDOWS):
        cs = slice(g * POOL_GROUP_DIM, (g + 1) * POOL_GROUP_DIM)
        a = a_ref[:, cs]
        band = jnp.where((col <= row) & (col > row - win), 1.0, 0.0).astype(BF16)
        pband = jnp.where(pcol > prow - win, 1.0, 0.0).astype(BF16)
        wsum = _dot(band, a) + _dot(pband, prev_ref[:, cs])
        cnt = jnp.minimum(t_abs + 1, win).astype(F32)
        diff = (wsum / cnt - a.astype(F32)).astype(BF16)
        o_ref[:, cs] = (_dot(diff, w_ref[g]) * scale_ref[:, cs]).astype(o_ref.dtype)
    prev_ref[...] = a_ref[t - POOL_HALO:, :]


def _pool_call(main, pool_w, pool_scale):
    b, s, _ = main.shape
    t = 256
    return pl.pallas_call(
        functools.partial(_pool_kernel, t=t),
        grid=(b, s // t),
        in_specs=[
            pl.BlockSpec((None, t, D_BRANCH), lambda i, j: (i, j, 0)),
            pl.BlockSpec(pool_w.shape, lambda i, j: (0, 0, 0)),
            pl.BlockSpec((1, D_BRANCH), lambda i, j: (0, 0)),
        ],
        out_specs=pl.BlockSpec((None, t, D_BRANCH), lambda i, j: (i, j, 0)),
        out_shape=jax.ShapeDtypeStruct((b, s, D_BRANCH), BF16),
        scratch_shapes=[pltpu.VMEM((POOL_HALO, D_BRANCH), BF16)],
        compiler_params=_params("parallel", "arbitrary"),
        name="pool_mixer",
    )(main, pool_w, pool_scale)


def _sb_kernel(q_ref, k_ref, v_ref, o_ref, *, tq):
    qb = pl.program_id(1)
    scale = SB_HEAD_DIM ** -0.5 * LOG2_E
    row = lax.broadcasted_iota(jnp.int32, (tq, tq), 0)
    col = lax.broadcasted_iota(jnp.int32, (tq, tq), 1)
    below = row > col
    after_sum = jnp.where(below, 1.0, 0.0).astype(BF16)

    def sweep(rows, accs, rs):
        diag = accs is None
        heads = [slice(h * SB_HEAD_DIM, (h + 1) * SB_HEAD_DIM) for h in range(SB_HEADS)]
        zs = [_dot_nt(q_ref[:, hs], k_ref[rows, hs]) * scale for hs in heads]
        log_betas, drops, his, los = [], [], [], []
        for z in zs:
            l = jnp.log(1.0 + jnp.exp2(-jnp.abs(z))) * LOG2_E
            drop = jnp.maximum(z, 0.0) + l
            if diag:
                drop = jnp.where(below, drop, 0.0)
            hi, lo = _split_bf16(drop, 2)
            log_betas.append(jnp.minimum(z, 0.0) - l)
            drops.append(drop)
            his.append(hi)
            los.append(lo)
        sums = _dot(jnp.concatenate(his + los, axis=0), after_sum)
        attns, new_rs, rmin = [], [], None
        for h in range(SB_HEADS):
            after = sums[h * tq:(h + 1) * tq] + sums[(SB_HEADS + h) * tq:(SB_HEADS + h + 1) * tq]
            rowsum = jnp.sum(drops[h], axis=1, keepdims=True)
            if diag:
                attn = jnp.where(below, jnp.exp2(log_betas[h] - after), 0.0)
                rn = rowsum
            else:
                attn = jnp.exp2(log_betas[h] - (after + rs[h]))
                rn = rs[h] + rowsum
            attns.append(attn.astype(BF16))
            new_rs.append(rn)
            rmin = rn if rmin is None else jnp.minimum(rmin, rn)
        pvs = [_dot(attns[h], v_ref[rows, hs]) for h, hs in enumerate(heads)]
        new_accs = pvs if diag else [a + p for a, p in zip(accs, pvs)]
        return tuple(new_accs), tuple(new_rs), jnp.min(rmin)

    accs0, rs0, rmin0 = sweep(pl.ds(pl.multiple_of(qb * tq, tq), tq), None, None)

    def cond(carry):
        kb, rmin, _, _ = carry
        return jnp.logical_and(kb >= 0, rmin < SB_ZERO_BITS)

    def body(carry):
        kb, _, accs, rs = carry
        accs, rs, rmin = sweep(pl.ds(pl.multiple_of(kb * tq, tq), tq), accs, rs)
        return kb - 1, rmin, accs, rs

    _, _, accs, _ = lax.while_loop(cond, body, (qb - 1, rmin0, accs0, rs0))
    for h in range(SB_HEADS):
        o_ref[:, h * SB_HEAD_DIM:(h + 1) * SB_HEAD_DIM] = accs[h].astype(o_ref.dtype)


def _sb_call(main):
    b, s, _ = main.shape
    tq = 128
    return pl.pallas_call(
        functools.partial(_sb_kernel, tq=tq),
        grid=(b, s // tq),
        in_specs=[
            pl.BlockSpec((None, tq, D_BRANCH), lambda i, j: (i, j, 1)),
            pl.BlockSpec((None, s, D_BRANCH), lambda i, j: (i, 0, 2)),
            pl.BlockSpec((None, s, D_BRANCH), lambda i, j: (i, 0, 3)),
        ],
        out_specs=pl.BlockSpec((None, tq, D_BRANCH), lambda i, j: (i, j, 0)),
        out_shape=jax.ShapeDtypeStruct((b, s, D_BRANCH), BF16),
        compiler_params=_params("parallel", "arbitrary"),
        name="stick_breaking",
    )(main, main, main)


def _mlstm_kernel(qk_ref, v_ref, og_ref, if_ref, ift_ref, cw_ref, bcol_ref, brow_ref, o_ref,
                  ext_ref, ct_ref, n_ref, m_ref, *, chunk):
    L = chunk

    @pl.when(pl.program_id(1) == 0)
    def _():
        ext_ref[0:CONV_HALO, :] = jnp.zeros((CONV_HALO, ext_ref.shape[1]), F32)
        ct_ref[...] = jnp.zeros_like(ct_ref)
        n_ref[...] = jnp.zeros_like(n_ref)
        m_ref[...] = jnp.zeros_like(m_ref)

    ext_ref[CONV_HALO:, :] = qk_ref[...].astype(F32)
    ext = ext_ref[...]
    cw = cw_ref[...]
    conv = ext * cw[ML_CONV - 1:ML_CONV, :]
    for j in range(1, ML_CONV):
        conv = conv + pltpu.roll(ext, j, axis=0) * cw[ML_CONV - 1 - j:ML_CONV - j, :]
    conv = conv[CONV_HALO:, :]
    qk = conv * jax.nn.sigmoid(conv)
    ext_ref[0:CONV_HALO, :] = ext_ref[L:L + CONV_HALO, :]

    if_c = if_ref[...] + bcol_ref[...]
    if_r = ift_ref[...] + brow_ref[...]
    logf_c = jnp.minimum(if_c, 0.0) - _log_sigmoid_parts(if_c)
    logf_r = jnp.minimum(if_r, 0.0) - _log_sigmoid_parts(if_r)
    row = lax.broadcasted_iota(jnp.int32, (L, L), 0)
    col = lax.broadcasted_iota(jnp.int32, (L, L), 1)
    causal = col <= row
    incl_lower = jnp.where(causal, 1.0, 0.0).astype(BF16)
    incl_upper = jnp.where(row <= col, 1.0, 0.0).astype(BF16)
    b_c = sum(_dot(incl_lower, p) for p in _split_bf16(logf_c, 3))
    b_r = sum(_dot(p, incl_upper) for p in _split_bf16(logf_r, 3))

    for h in range(ML_HEADS):
        hs = slice(h * ML_HEAD_DIM, (h + 1) * ML_HEAD_DIM)
        ks = slice(D_BRANCH + h * ML_HEAD_DIM, D_BRANCH + (h + 1) * ML_HEAD_DIM)
        bcol = b_c[:, ML_HEADS + h:ML_HEADS + h + 1]
        icol = if_c[:, h:h + 1]
        brow = b_r[ML_HEADS + h:ML_HEADS + h + 1, :]
        irow = if_r[h:h + 1, :]
        m_prev = m_ref[h:h + 1, 0:1]

        d = jnp.where(causal, bcol - brow + irow, -jnp.inf)
        inter = bcol + m_prev
        m_t = jnp.maximum(inter, jnp.max(d, axis=1, keepdims=True))
        w_intra = jnp.exp(d - m_t)
        w_inter = jnp.exp(inter - m_t)

        q = qk[:, hs].astype(BF16)
        k32 = qk[:, ks] * (ML_HEAD_DIM ** -0.5)
        v = v_ref[:, hs]
        ct = ct_ref[h]
        nrow = n_ref[h:h + 1, :]
        s = _dot_nt(q, k32.astype(BF16)) * w_intra
        num = w_inter * _dot(q, ct.astype(BF16)) + _dot(s.astype(BF16), v)
        den = (w_inter * jnp.sum(q.astype(F32) * nrow, axis=1, keepdims=True)
               + jnp.sum(s, axis=1, keepdims=True))
        hid = num / jnp.maximum(jnp.abs(den), jnp.exp(-m_t))
        og = og_ref[:, hs].astype(F32)
        o_ref[:, hs] = (hid * jax.nn.sigmoid(og)).astype(o_ref.dtype)

        b_last = bcol[L - 1:L, :]
        dec = b_last - bcol + icol
        m_new = jnp.maximum(b_last + m_prev, jnp.max(dec, axis=0, keepdims=True))
        w_s = jnp.exp(dec - m_new)
        w_prev = jnp.exp(b_last + m_prev - m_new)
        kw = k32 * w_s
        ct_ref[h] = w_prev * ct + _dot_tn(kw.astype(BF16), v)
        n_ref[h:h + 1, :] = w_prev * nrow + jnp.sum(kw, axis=0, keepdims=True)
        m_ref[h:h + 1, :] = jnp.broadcast_to(m_new, (1, m_ref.shape[1]))


def _mlstm_call(main, ifg, ifg_t, conv_w, bias_col, bias_row):
    b, s, _ = main.shape
    chunk = 256
    return pl.pallas_call(
        functools.partial(_mlstm_kernel, chunk=chunk),
        grid=(b, s // chunk),
        in_specs=[
            pl.BlockSpec((None, chunk, 2 * D_BRANCH), lambda i, j: (i, j, 2)),
            pl.BlockSpec((None, chunk, D_BRANCH), lambda i, j: (i, j, 6)),
            pl.BlockSpec((None, chunk, D_BRANCH), lambda i, j: (i, j, 7)),
            pl.BlockSpec((None, chunk, IF_PAD), lambda i, j: (i, j, 0)),
            pl.BlockSpec((None, 2 * ML_HEADS, chunk), lambda i, j: (i, 0, j)),
            pl.BlockSpec((ML_CONV, 2 * D_BRANCH), lambda i, j: (0, 0)),
            pl.BlockSpec((1, IF_PAD), lambda i, j: (0, 0)),
            pl.BlockSpec((2 * ML_HEADS, 1), lambda i, j: (0, 0)),
        ],
        out_specs=pl.BlockSpec((None, chunk, D_BRANCH), lambda i, j: (i, j, 0)),
        out_shape=jax.ShapeDtypeStruct((b, s, D_BRANCH), BF16),
        scratch_shapes=[
            pltpu.VMEM((chunk + CONV_HALO, 2 * D_BRANCH), F32),
            pltpu.VMEM((ML_HEADS, ML_HEAD_DIM, ML_HEAD_DIM), F32),
            pltpu.VMEM((2 * ML_HEADS, ML_HEAD_DIM), F32),
            pltpu.VMEM((2 * ML_HEADS, 128), F32),
        ],
        compiler_params=_params("parallel", "arbitrary"),
        name="mlstm",
    )(main, main, main, ifg, ifg_t, conv_w, bias_col, bias_row)


def _residual_ln(x, y, g, ln_g, ln_b, alpha):
    r = alpha * x + (1.0 + g) * y
    mu = jnp.mean(r, axis=-1, keepdims=True)
    cen = r - mu
    var = jnp.mean(cen * cen, axis=-1, keepdims=True)
    return cen * lax.rsqrt(var + LN_EPS) * ln_g + ln_b


def _mixer_out_kernel(yp_ref, ys_ref, ym_ref, g0_ref, g1_ref, g2_ref, wb_ref, wo_ref, x_ref, g_ref,
                      lng_ref, lnb_ref, o_ref, *, alpha):
    acc = None
    for i, (y_ref, gl_ref) in enumerate(((yp_ref, g0_ref), (ys_ref, g1_ref), (ym_ref, g2_ref))):
        term = jax.nn.sigmoid(gl_ref[...].astype(F32)) * _dot(y_ref[...], wb_ref[i])
        acc = term if acc is None else acc + term
    y = _dot(acc.astype(BF16), wo_ref[...])
    o_ref[...] = _residual_ln(x_ref[...], y, g_ref[...], lng_ref[...], lnb_ref[...], alpha)


def _resident(shape):
    return pl.BlockSpec(shape, lambda *_: (0,) * len(shape), pipeline_mode=pl.Buffered(1))


def _mixer_out_call(y_pool, y_sb, y_ml, gates, w_branch, w_out, x, g, ln_g, ln_b, alpha):
    b, s, d = x.shape
    tm = 256
    y_spec = pl.BlockSpec((None, tm, D_BRANCH), lambda i, j: (i, j, 0))

    def gate_spec(br):
        return pl.BlockSpec((None, tm, d), lambda i, j: (i, j, br))

    return pl.pallas_call(
        functools.partial(_mixer_out_kernel, alpha=alpha),
        grid=(b, s // tm),
        in_specs=[y_spec, y_spec, y_spec, gate_spec(0), gate_spec(1), gate_spec(2),
                  _resident(w_branch.shape), _resident(w_out.shape),
                  pl.BlockSpec((None, tm, d), lambda i, j: (i, j, 0)),
                  pl.BlockSpec((None, 1, d), lambda i, j: (i, 0, 0)),
                  pl.BlockSpec((1, d), lambda i, j: (0, 0)),
                  pl.BlockSpec((1, d), lambda i, j: (0, 0))],
        out_specs=pl.BlockSpec((None, tm, d), lambda i, j: (i, j, 0)),
        out_shape=jax.ShapeDtypeStruct((b, s, d), F32),
        compiler_params=_params("parallel", "parallel"),
        name="mixer_out_ln",
    )(y_pool, y_sb, y_ml, gates, gates, gates, w_branch, w_out, x, g, ln_g, ln_b)


def _ffn_kernel(x_ref, xh_ref, sc_ref, sh_ref, wu_ref, cf_ref, wd_ref, g_ref, lng_ref, lnb_ref, o_ref,
                h_ref, raw_old_ref, raw_new_ref, *, alpha, nff):
    n = pl.program_id(2)
    tn = wd_ref.shape[0]

    parts = FFN_PARTS
    up_w = 2 * tn // parts
    act_w = tn // parts

    def up_batch(j):
        cs = slice(j * up_w, (j + 1) * up_w)
        raw_new_ref[:, cs] = _dot(h_ref[...], wu_ref[:, cs])

    def conv(cs):
        raw = raw_old_ref[:, cs]
        c = cf_ref[:, cs]
        out = raw * c[FFN_CONV - 1:FFN_CONV, :]
        for j in range(1, FFN_CONV):
            out = out + pltpu.roll(raw, j, axis=0) * c[FFN_CONV - 1 - j:FFN_CONV - j, :]
        return out[ROW_HALO:, :]

    def gate_part(p):
        val = conv(slice(p * act_w, (p + 1) * act_w))
        gt = conv(slice(tn + p * act_w, tn + (p + 1) * act_w))
        return (gt * jax.nn.sigmoid(gt) * val).astype(BF16)

    def down_part(p):
        return _dot(gate_part(p), wd_ref[p * act_w:(p + 1) * act_w, :])

    @pl.when(n == 0)
    def _():
        sc = 1.0 + sc_ref[...]
        sh = sh_ref[...]
        halo = jnp.where(pl.program_id(1) > 0, xh_ref[...] * sc + sh, 0.0)
        h_ref[0:ROW_HALO, :] = halo.astype(h_ref.dtype)
        h_ref[ROW_HALO:, :] = (x_ref[...] * sc + sh).astype(h_ref.dtype)
        o_ref[...] = jnp.zeros_like(o_ref)
        for p in range(parts):
            up_batch(p)
        raw_old_ref[...] = raw_new_ref[...]

    @pl.when(jnp.logical_and(n >= 1, n < nff))
    def _():
        for p in range(parts):
            up_batch(p)
            o_ref[...] += down_part(p)
        raw_old_ref[...] = raw_new_ref[...]

    @pl.when(n == nff)
    def _():
        y = o_ref[...]
        for p in range(parts):
            y = y + down_part(p)
        o_ref[...] = _residual_ln(x_ref[...], y, g_ref[...], lng_ref[...], lnb_ref[...], alpha)


def _pack_val_gate(a, tn):
    r, two_dff = a.shape
    nff = two_dff // (2 * tn)
    return a.reshape(r, 2, nff, tn).transpose(0, 2, 1, 3).reshape(r, two_dff)


def _ffn_call(x, sc, sh, w_up, conv_ff, w_down, g, ln_g, ln_b, alpha):
    b, s, d = x.shape
    dff = w_down.shape[0]
    tm = _pick(s, (512, 256))
    tn = _pick(dff, (512, 256))
    nff = dff // tn
    assert nff >= 2
    halo_blocks = tm // ROW_HALO
    w_up = _pack_val_gate(w_up, tn).astype(BF16)
    conv_ff = _pack_val_gate(conv_ff, tn)

    def clamp(k):
        return jnp.clip(k, 0, nff - 1)

    return pl.pallas_call(
        functools.partial(_ffn_kernel, alpha=alpha, nff=nff),
        grid=(b, s // tm, nff + 1),
        in_specs=[
            pl.BlockSpec((None, tm, d), lambda i, j, k: (i, j, 0)),
            pl.BlockSpec((None, ROW_HALO, d), lambda i, j, k: (i, jnp.maximum(j * halo_blocks - 1, 0), 0)),
            pl.BlockSpec((None, 1, d), lambda i, j, k: (i, 0, 0)),
            pl.BlockSpec((None, 1, d), lambda i, j, k: (i, 0, 0)),
            pl.BlockSpec((d, 2 * tn), lambda i, j, k: (0, clamp(k))),
            pl.BlockSpec((FFN_CONV, 2 * tn), lambda i, j, k: (0, clamp(k - 1))),
            pl.BlockSpec((tn, d), lambda i, j, k: (clamp(k - 1), 0)),
            pl.BlockSpec((None, 1, d), lambda i, j, k: (i, 0, 0)),
            pl.BlockSpec((1, d), lambda i, j, k: (0, 0)),
            pl.BlockSpec((1, d), lambda i, j, k: (0, 0)),
        ],
        out_specs=pl.BlockSpec((None, tm, d), lambda i, j, k: (i, j, 0)),
        out_shape=jax.ShapeDtypeStruct((b, s, d), F32),
        scratch_shapes=[pltpu.VMEM((tm + ROW_HALO, d), BF16),
                        pltpu.VMEM((tm + ROW_HALO, 2 * tn), F32), pltpu.VMEM((tm + ROW_HALO, 2 * tn), F32)],
        compiler_params=_params("parallel", "parallel", "arbitrary"),
        name="ffn_conv_gate_down_ln",
    )(x, x, sc, sh, w_up, conv_ff, w_down, g, ln_g, ln_b)


def kernel(x, c, w_ada, b_ada, w_in, conv_ml, pool_w, pool_scale, ig_bias, fg_bias, w_branch, w_out,
           w_up, conv_ff, w_down, ln_g, ln_b):
    b, s, d = x.shape
    depth = w_in.shape[0]
    alpha = (2.0 * depth) ** 0.25

    mod = _mod_call(c, w_ada, b_ada)
    for l in range(depth):
        sh1, sc1, g1, sh2, sc2, g2 = (mod[l][:, None, i * d:(i + 1) * d] for i in range(6))

        w_mix = w_in[l][:, :IF_COL].astype(BF16)
        w_gate = w_in[l][:, IF_COL + 2 * ML_HEADS:].astype(BF16)
        w_if = jnp.pad(w_in[l][:, IF_COL:IF_COL + 2 * ML_HEADS], ((0, 0), (0, IF_PAD - 2 * ML_HEADS))).astype(BF16)
        main, gates, ifg = _inproj_call(x, sc1, sh1, w_mix, w_gate, w_if)
        ifg_t = jnp.swapaxes(ifg[:, :, :2 * ML_HEADS], 1, 2)
        gate_bias = jnp.concatenate([ig_bias[l], fg_bias[l]])
        bias_col = jnp.pad(gate_bias, (0, IF_PAD - 2 * ML_HEADS)).reshape(1, IF_PAD)
        bias_row = gate_bias.reshape(2 * ML_HEADS, 1)

        y_pool = _pool_call(main, pool_w[l].astype(BF16), pool_scale[l].reshape(1, D_BRANCH))
        y_sb = _sb_call(main)
        y_ml = _mlstm_call(main, ifg, ifg_t, conv_ml[l], bias_col, bias_row)
        x = _mixer_out_call(y_pool, y_sb, y_ml, gates, w_branch[l].astype(BF16), w_out[l].astype(BF16), x, g1,
                            ln_g[l, 0].reshape(1, d), ln_b[l, 0].reshape(1, d), alpha)
        x = _ffn_call(x, sc2, sh2, w_up[l], conv_ff[l], w_down[l].astype(BF16), g2,
                      ln_g[l, 1].reshape(1, d), ln_b[l, 1].reshape(1, d), alpha)
    return x
```

```python
import functools

import jax
import jax.numpy as jnp
from jax import lax
from jax.experimental import pallas as pl
from jax.experimental.pallas import tpu as pltpu

F32 = jnp.float32
BF16 = jnp.bfloat16

D_BRANCH = 1024
POOL_WINDOWS = (2, 4, 8, 16)
POOL_GROUP_DIM = 256
POOL_HALO = 16
SB_HEADS = 8
SB_HEAD_DIM = 128
ML_HEADS = 4
ML_HEAD_DIM = 256
ML_CONV = 4
FFN_CONV = 3
FFN_SUB = 256
LN_EPS = 1e-5
GATE_COL = 8 * D_BRANCH
IF_COL = 8 * D_BRANCH
IF_PAD = 128
ROW_HALO = 16
CONV_HALO = 8

LOG2_E = 1.4426950408889634

SB_ZERO_BITS = 127.0

VMEM_LIMIT_BYTES = 48 * 1024 * 1024


def _params(*sem):
    return pltpu.CompilerParams(dimension_semantics=sem, vmem_limit_bytes=VMEM_LIMIT_BYTES)


def _pick(n, cands):
    for c in cands:
        if n % c == 0:
            return c
    raise ValueError(f"no tile in {cands} divides {n}")


def _dot(a, b):
    return jnp.dot(a, b, preferred_element_type=F32)


def _dot_nt(a, b):
    return lax.dot_general(a, b, (((1,), (1,)), ((), ())), preferred_element_type=F32)


def _dot_tn(a, b):
    return lax.dot_general(a, b, (((0,), (0,)), ((), ())), preferred_element_type=F32)


def _log_sigmoid_parts(z):
    return jnp.log1p(jnp.exp(-jnp.abs(z)))


def _split_bf16(a, terms):
    parts = []
    rem = a
    for _ in range(terms):
        p = rem.astype(BF16)
        parts.append(p)
        rem = rem - p.astype(F32)
    return parts


def _mod_kernel(c_ref, w_ref, b_ref, o_ref):
    c = c_ref[...]
    ca = (c * jax.nn.sigmoid(c)).astype(BF16)
    o_ref[...] = _dot(ca, w_ref[...].astype(BF16)) + b_ref[...]


def _mod_call(c, w_ada, b_ada):
    depth, d, n = w_ada.shape
    b = c.shape[0]
    tn = _pick(n, (1024, 512, 256, 128))
    return pl.pallas_call(
        _mod_kernel,
        grid=(depth, n // tn),
        in_specs=[
            pl.BlockSpec((b, d), lambda l, j: (0, 0)),
            pl.BlockSpec((None, d, tn), lambda l, j: (l, 0, j)),
            pl.BlockSpec((None, 1, tn), lambda l, j: (l, 0, j)),
        ],
        out_specs=pl.BlockSpec((None, b, tn), lambda l, j: (l, 0, j)),
        out_shape=jax.ShapeDtypeStruct((depth, b, n), F32),
        compiler_params=_params("parallel", "parallel"),
        name="adaln_mod",
    )(c, w_ada, b_ada.reshape(depth, 1, n))


def _inproj_kernel(x_ref, sc_ref, sh_ref, w_ref, o_ref, h_ref):
    @pl.when(pl.program_id(2) == 0)
    def _():
        h_ref[...] = (x_ref[...] * (1.0 + sc_ref[...]) + sh_ref[...]).astype(h_ref.dtype)

    o_ref[...] = _dot(h_ref[...], w_ref[...]).astype(o_ref.dtype)


def _inproj_call(x, sc, sh, w, layer, out_dtype, name):
    b, s, d = x.shape
    n = w.shape[2]
    tm = _pick(s, (1024, 512, 256))
    tn = _pick(n, (1024, 512, 256, 128))
    return pl.pallas_call(
        _inproj_kernel,
        grid=(b, s // tm, n // tn),
        in_specs=[
            pl.BlockSpec((None, tm, d), lambda i, j, k: (i, j, 0)),
            pl.BlockSpec((None, 1, d), lambda i, j, k: (i, 0, 0)),
            pl.BlockSpec((None, 1, d), lambda i, j, k: (i, 0, 0)),
            pl.BlockSpec((None, d, tn), lambda i, j, k: (layer, 0, k)),
        ],
        out_specs=pl.BlockSpec((None, tm, tn), lambda i, j, k: (i, j, k)),
        out_shape=jax.ShapeDtypeStruct((b, s, n), out_dtype),
        scratch_shapes=[pltpu.VMEM((tm, d), BF16)],
        compiler_params=_params("parallel", "parallel", "arbitrary"),
        name=name,
    )(x, sc, sh, w)


def _pool_kernel(a_ref, w_ref, scale_ref, o_ref, prev_ref, *, t):
    st = pl.program_id(1)

    @pl.when(st == 0)
    def _():
        prev_ref[...] = jnp.zeros_like(prev_ref)

    row = lax.broadcasted_iota(jnp.int32, (t, t), 0)
    col = lax.broadcasted_iota(jnp.int32, (t, t), 1)
    prow = lax.broadcasted_iota(jnp.int32, (t, POOL_HALO), 0)
    pcol = lax.broadcasted_iota(jnp.int32, (t, POOL_HALO), 1) - POOL_HALO
    t_abs = st * t + lax.broadcasted_iota(jnp.int32, (t, 1), 0)
    for g, win in enumerate(POOL_WINDOWS):
        cs = slice(g * POOL_GROUP_DIM, (g + 1) * POOL_GROUP_DIM)
        a = a_ref[:, cs]
        band = jnp.where((col <= row) & (col > row - win), 1.0, 0.0).astype(BF16)
        pband = jnp.where(pcol > prow - win, 1.0, 0.0).astype(BF16)
        wsum = _dot(band, a) + _dot(pband, prev_ref[:, cs])
        cnt = jnp.minimum(t_abs + 1, win).astype(F32)
        diff = (wsum / cnt - a.astype(F32)).astype(BF16)
        o_ref[:, cs] = (_dot(diff, w_ref[g]) * scale_ref[:, cs]).astype(o_ref.dtype)
    prev_ref[...] = a_ref[t - POOL_HALO:, :]


def _pool_call(main, pool_w, pool_scale, layer):
    b, s, _ = main.shape
    t = 256
    return pl.pallas_call(
        functools.partial(_pool_kernel, t=t),
        grid=(b, s // t),
        in_specs=[
            pl.BlockSpec((None, t, D_BRANCH), lambda i, j: (i, j, 0)),
            pl.BlockSpec((None,) + pool_w.shape[1:], lambda i, j: (layer, 0, 0, 0)),
            pl.BlockSpec((None, 1, D_BRANCH), lambda i, j: (layer, 0, 0)),
        ],
        out_specs=pl.BlockSpec((None, t, D_BRANCH), lambda i, j: (i, j, 0)),
        out_shape=jax.ShapeDtypeStruct((b, s, D_BRANCH), BF16),
        scratch_shapes=[pltpu.VMEM((POOL_HALO, D_BRANCH), BF16)],
        compiler_params=_params("parallel", "arbitrary"),
        name="pool_mixer",
    )(main, pool_w, pool_scale)


def _sb_kernel(q_ref, k_ref, v_ref, o_ref, *, tq):
    qb = pl.program_id(1)
    scale = SB_HEAD_DIM ** -0.5 * LOG2_E
    row = lax.broadcasted_iota(jnp.int32, (tq, tq), 0)
    col = lax.broadcasted_iota(jnp.int32, (tq, tq), 1)
    below = row > col
    after_sum = jnp.where(below, 1.0, 0.0).astype(BF16)

    def sweep(rows, accs, rs):
        diag = accs is None
        heads = [slice(h * SB_HEAD_DIM, (h + 1) * SB_HEAD_DIM) for h in range(SB_HEADS)]
        zs = [_dot_nt(q_ref[:, hs], k_ref[rows, hs]) * scale for hs in heads]
        log_betas, drops, his, los = [], [], [], []
        for z in zs:
            l = jnp.log(1.0 + jnp.exp2(-jnp.abs(z))) * LOG2_E
            drop = jnp.maximum(z, 0.0) + l
            if diag:
                drop = jnp.where(below, drop, 0.0)
            hi, lo = _split_bf16(drop, 2)
            log_betas.append(jnp.minimum(z, 0.0) - l)
            drops.append(drop)
            his.append(hi)
            los.append(lo)
        sums = _dot(jnp.concatenate(his + los, axis=0), after_sum)
        attns, new_rs, rmin = [], [], None
        for h in range(SB_HEADS):
            after = sums[h * tq:(h + 1) * tq] + sums[(SB_HEADS + h) * tq:(SB_HEADS + h + 1) * tq]
            rowsum = jnp.sum(drops[h], axis=1, keepdims=True)
            if diag:
                attn = jnp.where(below, jnp.exp2(log_betas[h] - after), 0.0)
                rn = rowsum
            else:
                attn = jnp.exp2(log_betas[h] - (after + rs[h]))
                rn = rs[h] + rowsum
            attns.append(attn.astype(BF16))
            new_rs.append(rn)
            rmin = rn if rmin is None else jnp.minimum(rmin, rn)
        pvs = [_dot(attns[h], v_ref[rows, hs]) for h, hs in enumerate(heads)]
        new_accs = pvs if diag else [a + p for a, p in zip(accs, pvs)]
        return tuple(new_accs), tuple(new_rs), jnp.min(rmin)

    accs0, rs0, rmin0 = sweep(pl.ds(pl.multiple_of(qb * tq, tq), tq), None, None)

    def cond(carry):
        kb, rmin, _, _ = carry
        return jnp.logical_and(kb >= 0, rmin < SB_ZERO_BITS)

    def body(carry):
        kb, _, accs, rs = carry
        accs, rs, rmin = sweep(pl.ds(pl.multiple_of(kb * tq, tq), tq), accs, rs)
        return kb - 1, rmin, accs, rs

    _, _, accs, _ = lax.while_loop(cond, body, (qb - 1, rmin0, accs0, rs0))
    for h in range(SB_HEADS):
        o_ref[:, h * SB_HEAD_DIM:(h + 1) * SB_HEAD_DIM] = accs[h].astype(o_ref.dtype)


def _sb_call(main):
    b, s, _ = main.shape
    tq = 128
    return pl.pallas_call(
        functools.partial(_sb_kernel, tq=tq),
        grid=(b, s // tq),
        in_specs=[
            pl.BlockSpec((None, tq, D_BRANCH), lambda i, j: (i, j, 1)),
            pl.BlockSpec((None, s, D_BRANCH), lambda i, j: (i, 0, 2)),
            pl.BlockSpec((None, s, D_BRANCH), lambda i, j: (i, 0, 3)),
        ],
        out_specs=pl.BlockSpec((None, tq, D_BRANCH), lambda i, j: (i, j, 0)),
        out_shape=jax.ShapeDtypeStruct((b, s, D_BRANCH), BF16),
        compiler_params=_params("parallel", "arbitrary"),
        name="stick_breaking",
    )(main, main, main)


def _mlstm_kernel(qk_ref, v_ref, og_ref, if_ref, ift_ref, cw_ref, bcol_ref, brow_ref, o_ref,
                  ext_ref, ct_ref, n_ref, m_ref, *, chunk):
    L = chunk

    @pl.when(pl.program_id(1) == 0)
    def _():
        ext_ref[0:CONV_HALO, :] = jnp.zeros((CONV_HALO, ext_ref.shape[1]), F32)
        ct_ref[...] = jnp.zeros_like(ct_ref)
        n_ref[...] = jnp.zeros_like(n_ref)
        m_ref[...] = jnp.zeros_like(m_ref)

    ext_ref[CONV_HALO:, :] = qk_ref[...].astype(F32)
    ext = ext_ref[...]
    cw = cw_ref[...]
    conv = ext * cw[ML_CONV - 1:ML_CONV, :]
    for j in range(1, ML_CONV):
        conv = conv + pltpu.roll(ext, j, axis=0) * cw[ML_CONV - 1 - j:ML_CONV - j, :]
    conv = conv[CONV_HALO:, :]
    qk = conv * jax.nn.sigmoid(conv)
    ext_ref[0:CONV_HALO, :] = ext_ref[L:L + CONV_HALO, :]

    if_c = if_ref[...] + bcol_ref[...]
    if_r = ift_ref[...] + brow_ref[...]
    logf_c = jnp.minimum(if_c, 0.0) - _log_sigmoid_parts(if_c)
    logf_r = jnp.minimum(if_r, 0.0) - _log_sigmoid_parts(if_r)
    row = lax.broadcasted_iota(jnp.int32, (L, L), 0)
    col = lax.broadcasted_iota(jnp.int32, (L, L), 1)
    causal = col <= row
    incl_lower = jnp.where(causal, 1.0, 0.0).astype(BF16)
    incl_upper = jnp.where(row <= col, 1.0, 0.0).astype(BF16)
    b_c = sum(_dot(incl_lower, p) for p in _split_bf16(logf_c, 3))
    b_r = sum(_dot(p, incl_upper) for p in _split_bf16(logf_r, 3))

    for h in range(ML_HEADS):
        hs = slice(h * ML_HEAD_DIM, (h + 1) * ML_HEAD_DIM)
        ks = slice(D_BRANCH + h * ML_HEAD_DIM, D_BRANCH + (h + 1) * ML_HEAD_DIM)
        bcol = b_c[:, ML_HEADS + h:ML_HEADS + h + 1]
        icol = if_c[:, h:h + 1]
        brow = b_r[ML_HEADS + h:ML_HEADS + h + 1, :]
        irow = if_r[h:h + 1, :]
        m_prev = m_ref[h:h + 1, 0:1]

        d = jnp.where(causal, bcol - brow + irow, -jnp.inf)
        inter = bcol + m_prev
        m_t = jnp.maximum(inter, jnp.max(d, axis=1, keepdims=True))
        w_intra = jnp.exp(d - m_t)
        w_inter = jnp.exp(inter - m_t)

        q = qk[:, hs].astype(BF16)
        k32 = qk[:, ks] * (ML_HEAD_DIM ** -0.5)
        v = v_ref[:, hs]
        ct = ct_ref[h]
        nrow = n_ref[h:h + 1, :]
        s = _dot_nt(q, k32.astype(BF16)) * w_intra
        num = w_inter * _dot(q, ct.astype(BF16)) + _dot(s.astype(BF16), v)
        den = (w_inter * jnp.sum(q.astype(F32) * nrow, axis=1, keepdims=True)
               + jnp.sum(s, axis=1, keepdims=True))
        hid = num / jnp.maximum(jnp.abs(den), jnp.exp(-m_t))
        og = og_ref[:, hs].astype(F32)
        o_ref[:, hs] = (hid * jax.nn.sigmoid(og)).astype(o_ref.dtype)

        b_last = bcol[L - 1:L, :]
        dec = b_last - bcol + icol
        m_new = jnp.maximum(b_last + m_prev, jnp.max(dec, axis=0, keepdims=True))
        w_s = jnp.exp(dec - m_new)
        w_prev = jnp.exp(b_last + m_prev - m_new)
        kw = k32 * w_s
        ct_ref[h] = w_prev * ct + _dot_tn(kw.astype(BF16), v)
        n_ref[h:h + 1, :] = w_prev * nrow + jnp.sum(kw, axis=0, keepdims=True)
        m_ref[h:h + 1, :] = jnp.broadcast_to(m_new, (1, m_ref.shape[1]))


def _mlstm_call(main, ifg, ifg_t, conv_w, bias_col, bias_row):
    b, s, _ = main.shape
    chunk = 256
    return pl.pallas_call(
        functools.partial(_mlstm_kernel, chunk=chunk),
        grid=(b, s // chunk),
        in_specs=[
            pl.BlockSpec((None, chunk, 2 * D_BRANCH), lambda i, j: (i, j, 2)),
            pl.BlockSpec((None, chunk, D_BRANCH), lambda i, j: (i, j, 6)),
            pl.BlockSpec((None, chunk, D_BRANCH), lambda i, j: (i, j, 7)),
            pl.BlockSpec((None, chunk, IF_PAD), lambda i, j: (i, j, 0)),
            pl.BlockSpec((None, 2 * ML_HEADS, chunk), lambda i, j: (i, 0, j)),
            pl.BlockSpec((ML_CONV, 2 * D_BRANCH), lambda i, j: (0, 0)),
            pl.BlockSpec((1, IF_PAD), lambda i, j: (0, 0)),
            pl.BlockSpec((2 * ML_HEADS, 1), lambda i, j: (0, 0)),
        ],
        out_specs=pl.BlockSpec((None, chunk, D_BRANCH), lambda i, j: (i, j, 0)),
        out_shape=jax.ShapeDtypeStruct((b, s, D_BRANCH), BF16),
        scratch_shapes=[
            pltpu.VMEM((chunk + CONV_HALO, 2 * D_BRANCH), F32),
            pltpu.VMEM((ML_HEADS, ML_HEAD_DIM, ML_HEAD_DIM), F32),
            pltpu.VMEM((2 * ML_HEADS, ML_HEAD_DIM), F32),
            pltpu.VMEM((2 * ML_HEADS, 128), F32),
        ],
        compiler_params=_params("parallel", "arbitrary"),
        name="mlstm",
    )(main, main, main, ifg, ifg_t, conv_w, bias_col, bias_row)


def _residual_ln(x, y, g, ln_g, ln_b, alpha):
    r = alpha * x + (1.0 + g) * y
    mu = jnp.mean(r, axis=-1, keepdims=True)
    cen = r - mu
    var = jnp.mean(cen * cen, axis=-1, keepdims=True)
    return cen * lax.rsqrt(var + LN_EPS) * ln_g + ln_b


def _mixer_out_kernel(yp_ref, ys_ref, ym_ref, g0_ref, g1_ref, g2_ref, wb_ref, wo_ref, x_ref, g_ref,
                      lng_ref, lnb_ref, o_ref, *, alpha):
    acc = None
    for i, (y_ref, gl_ref) in enumerate(((yp_ref, g0_ref), (ys_ref, g1_ref), (ym_ref, g2_ref))):
        term = jax.nn.sigmoid(gl_ref[...].astype(F32)) * _dot(y_ref[...], wb_ref[i])
        acc = term if acc is None else acc + term
    y = _dot(acc.astype(BF16), wo_ref[...])
    o_ref[...] = _residual_ln(x_ref[...], y, g_ref[...], lng_ref[...], lnb_ref[...], alpha)


def _resident_layer(stacked_shape, layer):
    tail = (0,) * (len(stacked_shape) - 1)
    return pl.BlockSpec((None,) + tuple(stacked_shape[1:]), lambda *_: (layer,) + tail,
                        pipeline_mode=pl.Buffered(1))


def _mixer_out_call(y_pool, y_sb, y_ml, main, w_branch, w_out, x, g, ln_g, ln_b, alpha, layer):
    b, s, d = x.shape
    tm = 256
    y_spec = pl.BlockSpec((None, tm, D_BRANCH), lambda i, j: (i, j, 0))

    def gate_spec(br):
        blk = GATE_COL // d + br
        return pl.BlockSpec((None, tm, d), lambda i, j: (i, j, blk))

    return pl.pallas_call(
        functools.partial(_mixer_out_kernel, alpha=alpha),
        grid=(b, s // tm),
        in_specs=[y_spec, y_spec, y_spec, gate_spec(0), gate_spec(1), gate_spec(2),
                  _resident_layer(w_branch.shape, layer), _resident_layer(w_out.shape, layer),
                  pl.BlockSpec((None, tm, d), lambda i, j: (i, j, 0)),
                  pl.BlockSpec((None, 1, d), lambda i, j: (i, 0, 0)),
                  pl.BlockSpec((1, d), lambda i, j: (0, 0)),
                  pl.BlockSpec((1, d), lambda i, j: (0, 0))],
        out_specs=pl.BlockSpec((None, tm, d), lambda i, j: (i, j, 0)),
        out_shape=jax.ShapeDtypeStruct((b, s, d), F32),
        compiler_params=_params("parallel", "parallel"),
        name="mixer_out_ln",
    )(y_pool, y_sb, y_ml, main, main, main, w_branch, w_out, x, g, ln_g, ln_b)


def _ffn_kernel(x_ref, xh_ref, sc_ref, sh_ref, wv_ref, wg_ref, cv_ref, cg_ref, wd_ref, g_ref,
                lng_ref, lnb_ref, o_ref, h_ref, *, alpha):
    n = pl.program_id(2)

    @pl.when(n == 0)
    def _():
        sc = 1.0 + sc_ref[...]
        sh = sh_ref[...]
        halo = jnp.where(pl.program_id(1) > 0, xh_ref[...] * sc + sh, 0.0)
        h_ref[0:ROW_HALO, :] = halo.astype(h_ref.dtype)
        h_ref[ROW_HALO:, :] = (x_ref[...] * sc + sh).astype(h_ref.dtype)
        o_ref[...] = jnp.zeros_like(o_ref)

    h = h_ref[...]

    def conv(up, c_ref, cs):
        c = c_ref[:, cs]
        out = up * c[FFN_CONV - 1:FFN_CONV, :]
        for j in range(1, FFN_CONV):
            out = out + pltpu.roll(up, j, axis=0) * c[FFN_CONV - 1 - j:FFN_CONV - j, :]
        return out[ROW_HALO:, :]

    subs = [slice(j * FFN_SUB, (j + 1) * FFN_SUB) for j in range(wv_ref.shape[1] // FFN_SUB)]
    ups = [(_dot(h, wv_ref[:, cs]), _dot(h, wg_ref[:, cs])) for cs in subs]
    part = None
    for cs, (up_val, up_gate) in zip(subs, ups):
        val = conv(up_val, cv_ref, cs)
        gate = conv(up_gate, cg_ref, cs)
        act = (gate * jax.nn.sigmoid(gate) * val).astype(BF16)
        down = _dot(act, wd_ref[cs, :])
        part = down if part is None else part + down
    o_ref[...] += part

    @pl.when(n == pl.num_programs(2) - 1)
    def _():
        o_ref[...] = _residual_ln(x_ref[...], o_ref[...], g_ref[...], lng_ref[...], lnb_ref[...], alpha)


def _ffn_call(x, sc, sh, w_up, conv_ff, w_down, g, ln_g, ln_b, alpha, layer):
    b, s, d = x.shape
    dff = w_down.shape[1]
    tm = _pick(s, (512, 256))
    tn = _pick(dff, (512, 256))
    nff = dff // tn
    halo_blocks = tm // ROW_HALO
    return pl.pallas_call(
        functools.partial(_ffn_kernel, alpha=alpha),
        grid=(b, s // tm, nff),
        in_specs=[
            pl.BlockSpec((None, tm, d), lambda i, j, k: (i, j, 0)),
            pl.BlockSpec((None, ROW_HALO, d), lambda i, j, k: (i, jnp.maximum(j * halo_blocks - 1, 0), 0)),
            pl.BlockSpec((None, 1, d), lambda i, j, k: (i, 0, 0)),
            pl.BlockSpec((None, 1, d), lambda i, j, k: (i, 0, 0)),
            pl.BlockSpec((None, d, tn), lambda i, j, k: (layer, 0, k)),
            pl.BlockSpec((None, d, tn), lambda i, j, k: (layer, 0, nff + k)),
            pl.BlockSpec((None, FFN_CONV, tn), lambda i, j, k: (layer, 0, k)),
            pl.BlockSpec((None, FFN_CONV, tn), lambda i, j, k: (layer, 0, nff + k)),
            pl.BlockSpec((None, tn, d), lambda i, j, k: (layer, k, 0)),
            pl.BlockSpec((None, 1, d), lambda i, j, k: (i, 0, 0)),
            pl.BlockSpec((1, d), lambda i, j, k: (0, 0)),
            pl.BlockSpec((1, d), lambda i, j, k: (0, 0)),
        ],
        out_specs=pl.BlockSpec((None, tm, d), lambda i, j, k: (i, j, 0)),
        out_shape=jax.ShapeDtypeStruct((b, s, d), F32),
        scratch_shapes=[pltpu.VMEM((tm + ROW_HALO, d), BF16)],
        compiler_params=_params("parallel", "parallel", "arbitrary"),
        name="ffn_conv_gate_down_ln",
    )(x, x, sc, sh, w_up, w_up, conv_ff, conv_ff, w_down, g, ln_g, ln_b)


def kernel(x, c, w_ada, b_ada, w_in, conv_ml, pool_w, pool_scale, ig_bias, fg_bias, w_branch, w_out,
           w_up, conv_ff, w_down, ln_g, ln_b):
    b, s, d = x.shape
    depth = w_in.shape[0]
    alpha = (2.0 * depth) ** 0.25

    w_main = jnp.concatenate([w_in[:, :, :IF_COL], w_in[:, :, IF_COL + 2 * ML_HEADS:]], axis=2).astype(BF16)
    w_if = jnp.pad(w_in[:, :, IF_COL:IF_COL + 2 * ML_HEADS],
                   ((0, 0), (0, 0), (0, IF_PAD - 2 * ML_HEADS))).astype(BF16)
    pool_w, w_branch, w_out, w_up, w_down = (a.astype(BF16) for a in (pool_w, w_branch, w_out, w_up, w_down))
    pool_scale = pool_scale.reshape(depth, 1, D_BRANCH)

    mod = _mod_call(c, w_ada, b_ada)
    for l in range(depth):
        sh1, sc1, g1, sh2, sc2, g2 = (mod[l][:, None, i * d:(i + 1) * d] for i in range(6))

        main = _inproj_call(x, sc1, sh1, w_main, l, BF16, "in_proj")
        ifg = _inproj_call(x, sc1, sh1, w_if, l, F32, "in_proj_gates")
        ifg_t = jnp.swapaxes(ifg[:, :, :2 * ML_HEADS], 1, 2)
        gate_bias = jnp.concatenate([ig_bias[l], fg_bias[l]])
        bias_col = jnp.pad(gate_bias, (0, IF_PAD - 2 * ML_HEADS)).reshape(1, IF_PAD)
        bias_row = gate_bias.reshape(2 * ML_HEADS, 1)

        y_pool = _pool_call(main, pool_w, pool_scale, l)
        y_sb = _sb_call(main)
        y_ml = _mlstm_call(main, ifg, ifg_t, conv_ml[l], bias_col, bias_row)
        x = _mixer_out_call(y_pool, y_sb, y_ml, main, w_branch, w_out, x, g1,
                            ln_g[l, 0].reshape(1, d), ln_b[l, 0].reshape(1, d), alpha, l)
        x = _ffn_call(x, sc2, sh2, w_up, conv_ff, w_down, g2,
                      ln_g[l, 1].reshape(1, d), ln_b[l, 1].reshape(1, d), alpha, l)
    return x
```

```python
import functools

import jax
import jax.numpy as jnp
from jax import lax
from jax.experimental import pallas as pl
from jax.experimental.pallas import tpu as pltpu

F32 = jnp.float32
BF16 = jnp.bfloat16

D_BRANCH = 1024
POOL_WINDOWS = (2, 4, 8, 16)
POOL_GROUP_DIM = 256
POOL_HALO = 16
SB_HEADS = 8
SB_HEAD_DIM = 128
ML_HEADS = 4
ML_HEAD_DIM = 256
ML_CONV = 4
FFN_CONV = 3
FFN_SUB = 256
LN_EPS = 1e-5
GATE_COL = 8 * D_BRANCH
IF_COL = 8 * D_BRANCH
IF_PAD = 128
ROW_HALO = 16
CONV_HALO = 8

LOG2_E = 1.4426950408889634

SB_ZERO_BITS = 127.0

VMEM_LIMIT_BYTES = 48 * 1024 * 1024


def _params(*sem):
    return pltpu.CompilerParams(dimension_semantics=sem, vmem_limit_bytes=VMEM_LIMIT_BYTES)


def _pick(n, cands):
    for c in cands:
        if n % c == 0:
            return c
    raise ValueError(f"no tile in {cands} divides {n}")


def _dot(a, b):
    return jnp.dot(a, b, preferred_element_type=F32)


def _dot_nt(a, b):
    return lax.dot_general(a, b, (((1,), (1,)), ((), ())), preferred_element_type=F32)


def _dot_tn(a, b):
    return lax.dot_general(a, b, (((0,), (0,)), ((), ())), preferred_element_type=F32)


def _log_sigmoid_parts(z):
    return jnp.log1p(jnp.exp(-jnp.abs(z)))


def _split_bf16(a, terms):
    parts = []
    rem = a
    for _ in range(terms):
        p = rem.astype(BF16)
        parts.append(p)
        rem = rem - p.astype(F32)
    return parts


def _mod_kernel(c_ref, w_ref, b_ref, o_ref):
    c = c_ref[...]
    ca = (c * jax.nn.sigmoid(c)).astype(BF16)
    o_ref[...] = _dot(ca, w_ref[...].astype(BF16)) + b_ref[...]


def _mod_call(c, w_ada, b_ada):
    depth, d, n = w_ada.shape
    b = c.shape[0]
    tn = _pick(n, (1024, 512, 256, 128))
    return pl.pallas_call(
        _mod_kernel,
        grid=(depth, n // tn),
        in_specs=[
            pl.BlockSpec((b, d), lambda l, j: (0, 0)),
            pl.BlockSpec((None, d, tn), lambda l, j: (l, 0, j)),
            pl.BlockSpec((None, 1, tn), lambda l, j: (l, 0, j)),
        ],
        out_specs=pl.BlockSpec((None, b, tn), lambda l, j: (l, 0, j)),
        out_shape=jax.ShapeDtypeStruct((depth, b, n), F32),
        compiler_params=_params("parallel", "parallel"),
        name="adaln_mod",
    )(c, w_ada, b_ada.reshape(depth, 1, n))


def _inproj_kernel(x_ref, sc_ref, sh_ref, w_ref, o_ref, h_ref):
    @pl.when(pl.program_id(2) == 0)
    def _():
        h_ref[...] = (x_ref[...] * (1.0 + sc_ref[...]) + sh_ref[...]).astype(h_ref.dtype)

    o_ref[...] = _dot(h_ref[...], w_ref[...]).astype(o_ref.dtype)


def _inproj_call(x, sc, sh, w, layer, out_dtype, name):
    b, s, d = x.shape
    n = w.shape[2]
    tm = _pick(s, (1024, 512, 256))
    tn = _pick(n, (1024, 512, 256, 128))
    return pl.pallas_call(
        _inproj_kernel,
        grid=(b, s // tm, n // tn),
        in_specs=[
            pl.BlockSpec((None, tm, d), lambda i, j, k: (i, j, 0)),
            pl.BlockSpec((None, 1, d), lambda i, j, k: (i, 0, 0)),
            pl.BlockSpec((None, 1, d), lambda i, j, k: (i, 0, 0)),
            pl.BlockSpec((None, d, tn), lambda i, j, k: (layer, 0, k)),
        ],
        out_specs=pl.BlockSpec((None, tm, tn), lambda i, j, k: (i, j, k)),
        out_shape=jax.ShapeDtypeStruct((b, s, n), out_dtype),
        scratch_shapes=[pltpu.VMEM((tm, d), BF16)],
        compiler_params=_params("parallel", "parallel", "arbitrary"),
        name=name,
    )(x, sc, sh, w)


def _pool_tile(a_ref, w_ref, scale_ref, prev_ref, st):
    t = a_ref.shape[0]

    @pl.when(st == 0)
    def _():
        prev_ref[...] = jnp.zeros_like(prev_ref)

    row = lax.broadcasted_iota(jnp.int32, (t, t), 0)
    col = lax.broadcasted_iota(jnp.int32, (t, t), 1)
    prow = lax.broadcasted_iota(jnp.int32, (t, POOL_HALO), 0)
    pcol = lax.broadcasted_iota(jnp.int32, (t, POOL_HALO), 1) - POOL_HALO
    t_abs = st * t + lax.broadcasted_iota(jnp.int32, (t, 1), 0)
    outs = []
    for g, win in enumerate(POOL_WINDOWS):
        cs = slice(g * POOL_GROUP_DIM, (g + 1) * POOL_GROUP_DIM)
        a = a_ref[:, cs]
        band = jnp.where((col <= row) & (col > row - win), 1.0, 0.0).astype(BF16)
        pband = jnp.where(pcol > prow - win, 1.0, 0.0).astype(BF16)
        wsum = _dot(band, a) + _dot(pband, prev_ref[:, cs])
        cnt = jnp.minimum(t_abs + 1, win).astype(F32)
        diff = (wsum / cnt - a.astype(F32)).astype(BF16)
        outs.append((_dot(diff, w_ref[g]) * scale_ref[:, cs]).astype(BF16))
    prev_ref[...] = a_ref[t - POOL_HALO:, :]
    return jnp.concatenate(outs, axis=1)


def _sb_kernel(q_ref, k_ref, v_ref, o_ref, *, tq):
    qb = pl.program_id(1)
    scale = SB_HEAD_DIM ** -0.5 * LOG2_E
    row = lax.broadcasted_iota(jnp.int32, (tq, tq), 0)
    col = lax.broadcasted_iota(jnp.int32, (tq, tq), 1)
    below = row > col
    after_sum = jnp.where(below, 1.0, 0.0).astype(BF16)

    def sweep(rows, accs, rs):
        diag = accs is None
        heads = [slice(h * SB_HEAD_DIM, (h + 1) * SB_HEAD_DIM) for h in range(SB_HEADS)]
        zs = [_dot_nt(q_ref[:, hs], k_ref[rows, hs]) * scale for hs in heads]
        log_betas, drops, his, los = [], [], [], []
        for z in zs:
            l = jnp.log2(1.0 + jnp.exp2(-jnp.abs(z)))
            drop = jnp.maximum(z, 0.0) + l
            if diag:
                drop = jnp.where(below, drop, 0.0)
            hi, lo = _split_bf16(drop, 2)
            log_betas.append(jnp.minimum(z, 0.0) - l)
            drops.append(drop)
            his.append(hi)
            los.append(lo)
        sums = _dot(jnp.concatenate(his + los, axis=0), after_sum)
        attns, new_rs, rmin = [], [], None
        for h in range(SB_HEADS):
            after = sums[h * tq:(h + 1) * tq] + sums[(SB_HEADS + h) * tq:(SB_HEADS + h + 1) * tq]
            rowsum = jnp.sum(drops[h], axis=1, keepdims=True)
            if diag:
                attn = jnp.where(below, jnp.exp2(log_betas[h] - after), 0.0)
                rn = rowsum
            else:
                attn = jnp.exp2(log_betas[h] - (after + rs[h]))
                rn = rs[h] + rowsum
            attns.append(attn.astype(BF16))
            new_rs.append(rn)
            rmin = rn if rmin is None else jnp.minimum(rmin, rn)
        pvs = [_dot(attns[h], v_ref[rows, hs]) for h, hs in enumerate(heads)]
        new_accs = pvs if diag else [a + p for a, p in zip(accs, pvs)]
        return tuple(new_accs), tuple(new_rs), jnp.min(rmin)

    accs0, rs0, rmin0 = sweep(pl.ds(pl.multiple_of(qb * tq, tq), tq), None, None)

    def cond(carry):
        kb, rmin, _, _ = carry
        return jnp.logical_and(kb >= 0, rmin < SB_ZERO_BITS)

    def body(carry):
        kb, _, accs, rs = carry
        accs, rs, rmin = sweep(pl.ds(pl.multiple_of(kb * tq, tq), tq), accs, rs)
        return kb - 1, rmin, accs, rs

    _, _, accs, _ = lax.while_loop(cond, body, (qb - 1, rmin0, accs0, rs0))
    for h in range(SB_HEADS):
        o_ref[:, h * SB_HEAD_DIM:(h + 1) * SB_HEAD_DIM] = accs[h].astype(o_ref.dtype)


def _sb_call(main):
    b, s, _ = main.shape
    tq = 128
    return pl.pallas_call(
        functools.partial(_sb_kernel, tq=tq),
        grid=(b, s // tq),
        in_specs=[
            pl.BlockSpec((None, tq, D_BRANCH), lambda i, j: (i, j, 1)),
            pl.BlockSpec((None, s, D_BRANCH), lambda i, j: (i, 0, 2)),
            pl.BlockSpec((None, s, D_BRANCH), lambda i, j: (i, 0, 3)),
        ],
        out_specs=pl.BlockSpec((None, tq, D_BRANCH), lambda i, j: (i, j, 0)),
        out_shape=jax.ShapeDtypeStruct((b, s, D_BRANCH), BF16),
        compiler_params=_params("parallel", "arbitrary"),
        name="stick_breaking",
    )(main, main, main)


def _mlstm_kernel(qk_ref, v_ref, og_ref, if_ref, ift_ref, cw_ref, bcol_ref, brow_ref, o_ref,
                  ext_ref, ct_ref, n_ref, m_ref, *, chunk):
    L = chunk

    @pl.when(pl.program_id(1) == 0)
    def _():
        ext_ref[0:CONV_HALO, :] = jnp.zeros((CONV_HALO, ext_ref.shape[1]), F32)
        ct_ref[...] = jnp.zeros_like(ct_ref)
        n_ref[...] = jnp.zeros_like(n_ref)
        m_ref[...] = jnp.zeros_like(m_ref)

    ext_ref[CONV_HALO:, :] = qk_ref[...].astype(F32)
    ext = ext_ref[...]
    cw = cw_ref[...]
    conv = ext * cw[ML_CONV - 1:ML_CONV, :]
    for j in range(1, ML_CONV):
        conv = conv + pltpu.roll(ext, j, axis=0) * cw[ML_CONV - 1 - j:ML_CONV - j, :]
    conv = conv[CONV_HALO:, :]
    qk = conv * jax.nn.sigmoid(conv)
    ext_ref[0:CONV_HALO, :] = ext_ref[L:L + CONV_HALO, :]

    if_c = if_ref[...] + bcol_ref[...]
    if_r = ift_ref[...] + brow_ref[...]
    logf_c = jnp.minimum(if_c, 0.0) - _log_sigmoid_parts(if_c)
    logf_r = jnp.minimum(if_r, 0.0) - _log_sigmoid_parts(if_r)
    row = lax.broadcasted_iota(jnp.int32, (L, L), 0)
    col = lax.broadcasted_iota(jnp.int32, (L, L), 1)
    causal = col <= row
    incl_lower = jnp.where(causal, 1.0, 0.0).astype(BF16)
    incl_upper = jnp.where(row <= col, 1.0, 0.0).astype(BF16)
    b_c = sum(_dot(incl_lower, p) for p in _split_bf16(logf_c, 3))
    b_r = sum(_dot(p, incl_upper) for p in _split_bf16(logf_r, 3))

    for h in range(ML_HEADS):
        hs = slice(h * ML_HEAD_DIM, (h + 1) * ML_HEAD_DIM)
        ks = slice(D_BRANCH + h * ML_HEAD_DIM, D_BRANCH + (h + 1) * ML_HEAD_DIM)
        bcol = b_c[:, ML_HEADS + h:ML_HEADS + h + 1]
        icol = if_c[:, h:h + 1]
        brow = b_r[ML_HEADS + h:ML_HEADS + h + 1, :]
        irow = if_r[h:h + 1, :]
        m_prev = m_ref[h:h + 1, 0:1]

        d = jnp.where(causal, bcol - brow + irow, -jnp.inf)
        inter = bcol + m_prev
        m_t = jnp.maximum(inter, jnp.max(d, axis=1, keepdims=True))
        w_intra = jnp.exp(d - m_t)
        w_inter = jnp.exp(inter - m_t)

        q = qk[:, hs].astype(BF16)
        k32 = qk[:, ks] * (ML_HEAD_DIM ** -0.5)
        v = v_ref[:, hs]
        ct = ct_ref[h]
        nrow = n_ref[h:h + 1, :]
        s = _dot_nt(q, k32.astype(BF16)) * w_intra
        num = w_inter * _dot(q, ct.astype(BF16)) + _dot(s.astype(BF16), v)
        den = (w_inter * jnp.sum(q.astype(F32) * nrow, axis=1, keepdims=True)
               + jnp.sum(s, axis=1, keepdims=True))
        hid = num / jnp.maximum(jnp.abs(den), jnp.exp(-m_t))
        og = og_ref[:, hs].astype(F32)
        o_ref[:, hs] = (hid * jax.nn.sigmoid(og)).astype(o_ref.dtype)

        b_last = bcol[L - 1:L, :]
        dec = b_last - bcol + icol
        m_new = jnp.maximum(b_last + m_prev, jnp.max(dec, axis=0, keepdims=True))
        w_s = jnp.exp(dec - m_new)
        w_prev = jnp.exp(b_last + m_prev - m_new)
        kw = k32 * w_s
        ct_ref[h] = w_prev * ct + _dot_tn(kw.astype(BF16), v)
        n_ref[h:h + 1, :] = w_prev * nrow + jnp.sum(kw, axis=0, keepdims=True)
        m_ref[h:h + 1, :] = jnp.broadcast_to(m_new, (1, m_ref.shape[1]))


def _mlstm_call(main, ifg, ifg_t, conv_w, bias_col, bias_row):
    b, s, _ = main.shape
    chunk = 256
    return pl.pallas_call(
        functools.partial(_mlstm_kernel, chunk=chunk),
        grid=(b, s // chunk),
        in_specs=[
            pl.BlockSpec((None, chunk, 2 * D_BRANCH), lambda i, j: (i, j, 2)),
            pl.BlockSpec((None, chunk, D_BRANCH), lambda i, j: (i, j, 6)),
            pl.BlockSpec((None, chunk, D_BRANCH), lambda i, j: (i, j, 7)),
            pl.BlockSpec((None, chunk, IF_PAD), lambda i, j: (i, j, 0)),
            pl.BlockSpec((None, 2 * ML_HEADS, chunk), lambda i, j: (i, 0, j)),
            pl.BlockSpec((ML_CONV, 2 * D_BRANCH), lambda i, j: (0, 0)),
            pl.BlockSpec((1, IF_PAD), lambda i, j: (0, 0)),
            pl.BlockSpec((2 * ML_HEADS, 1), lambda i, j: (0, 0)),
        ],
        out_specs=pl.BlockSpec((None, chunk, D_BRANCH), lambda i, j: (i, j, 0)),
        out_shape=jax.ShapeDtypeStruct((b, s, D_BRANCH), BF16),
        scratch_shapes=[
            pltpu.VMEM((chunk + CONV_HALO, 2 * D_BRANCH), F32),
            pltpu.VMEM((ML_HEADS, ML_HEAD_DIM, ML_HEAD_DIM), F32),
            pltpu.VMEM((2 * ML_HEADS, ML_HEAD_DIM), F32),
            pltpu.VMEM((2 * ML_HEADS, 128), F32),
        ],
        compiler_params=_params("parallel", "arbitrary"),
        name="mlstm",
    )(main, main, main, ifg, ifg_t, conv_w, bias_col, bias_row)


def _residual_ln(x, y, g, ln_g, ln_b, alpha):
    r = alpha * x + (1.0 + g) * y
    mu = jnp.mean(r, axis=-1, keepdims=True)
    cen = r - mu
    var = jnp.mean(cen * cen, axis=-1, keepdims=True)
    return cen * lax.rsqrt(var + LN_EPS) * ln_g + ln_b


def _mixer_out_kernel(ap_ref, pw_ref, ps_ref, ys_ref, ym_ref, g0_ref, g1_ref, g2_ref, wb_ref, wo_ref, x_ref,
                      g_ref, lng_ref, lnb_ref, o_ref, prev_ref, *, alpha):
    y_pool = _pool_tile(ap_ref, pw_ref, ps_ref, prev_ref, pl.program_id(1))
    acc = None
    for i, (y, gl_ref) in enumerate(((y_pool, g0_ref), (ys_ref[...], g1_ref), (ym_ref[...], g2_ref))):
        term = jax.nn.sigmoid(gl_ref[...].astype(F32)) * _dot(y, wb_ref[i])
        acc = term if acc is None else acc + term
    y = _dot(acc.astype(BF16), wo_ref[...])
    o_ref[...] = _residual_ln(x_ref[...], y, g_ref[...], lng_ref[...], lnb_ref[...], alpha)


def _resident_layer(stacked_shape, layer):
    tail = (0,) * (len(stacked_shape) - 1)
    return pl.BlockSpec((None,) + tuple(stacked_shape[1:]), lambda *_: (layer,) + tail,
                        pipeline_mode=pl.Buffered(1))


def _mixer_out_call(main, pool_w, pool_scale, y_sb, y_ml, w_branch, w_out, x, g, ln_g, ln_b, alpha, layer):
    b, s, d = x.shape
    tm = 256
    y_spec = pl.BlockSpec((None, tm, D_BRANCH), lambda i, j: (i, j, 0))

    def gate_spec(br):
        blk = GATE_COL // d + br
        return pl.BlockSpec((None, tm, d), lambda i, j: (i, j, blk))

    return pl.pallas_call(
        functools.partial(_mixer_out_kernel, alpha=alpha),
        grid=(b, s // tm),
        in_specs=[y_spec, _resident_layer(pool_w.shape, layer), _resident_layer(pool_scale.shape, layer),
                  y_spec, y_spec, gate_spec(0), gate_spec(1), gate_spec(2),
                  _resident_layer(w_branch.shape, layer), _resident_layer(w_out.shape, layer),
                  pl.BlockSpec((None, tm, d), lambda i, j: (i, j, 0)),
                  pl.BlockSpec((None, 1, d), lambda i, j: (i, 0, 0)),
                  pl.BlockSpec((1, d), lambda i, j: (0, 0)),
                  pl.BlockSpec((1, d), lambda i, j: (0, 0))],
        out_specs=pl.BlockSpec((None, tm, d), lambda i, j: (i, j, 0)),
        out_shape=jax.ShapeDtypeStruct((b, s, d), F32),
        scratch_shapes=[pltpu.VMEM((POOL_HALO, D_BRANCH), BF16)],
        compiler_params=_params("parallel", "arbitrary"),
        name="mixer_out_ln",
    )(main, pool_w, pool_scale, y_sb, y_ml, main, main, main, w_branch, w_out, x, g, ln_g, ln_b)


def _ffn_kernel(x_ref, xh_ref, sc_ref, sh_ref, wv_ref, wg_ref, cv_ref, cg_ref, wd_ref, g_ref,
                lng_ref, lnb_ref, o_ref, h_ref, *, alpha):
    n = pl.program_id(2)

    @pl.when(n == 0)
    def _():
        sc = 1.0 + sc_ref[...]
        sh = sh_ref[...]
        halo = jnp.where(pl.program_id(1) > 0, xh_ref[...] * sc + sh, 0.0)
        h_ref[0:ROW_HALO, :] = halo.astype(h_ref.dtype)
        h_ref[ROW_HALO:, :] = (x_ref[...] * sc + sh).astype(h_ref.dtype)
        o_ref[...] = jnp.zeros_like(o_ref)

    h = h_ref[...]

    def conv(up, c_ref, cs):
        c = c_ref[:, cs]
        out = up * c[FFN_CONV - 1:FFN_CONV, :]
        for j in range(1, FFN_CONV):
            out = out + pltpu.roll(up, j, axis=0) * c[FFN_CONV - 1 - j:FFN_CONV - j, :]
        return out[ROW_HALO:, :]

    subs = [slice(j * FFN_SUB, (j + 1) * FFN_SUB) for j in range(wv_ref.shape[1] // FFN_SUB)]
    ups = [(_dot(h, wv_ref[:, cs]), _dot(h, wg_ref[:, cs])) for cs in subs]
    part = None
    for cs, (up_val, up_gate) in zip(subs, ups):
        val = conv(up_val, cv_ref, cs)
        gate = conv(up_gate, cg_ref, cs)
        act = (gate * jax.nn.sigmoid(gate) * val).astype(BF16)
        down = _dot(act, wd_ref[cs, :])
        part = down if part is None else part + down
    o_ref[...] += part

    @pl.when(n == pl.num_programs(2) - 1)
    def _():
        o_ref[...] = _residual_ln(x_ref[...], o_ref[...], g_ref[...], lng_ref[...], lnb_ref[...], alpha)


def _ffn_call(x, sc, sh, w_up, conv_ff, w_down, g, ln_g, ln_b, alpha, layer):
    b, s, d = x.shape
    dff = w_down.shape[1]
    tm = _pick(s, (512, 256))
    tn = _pick(dff, (512, 256))
    nff = dff // tn
    halo_blocks = tm // ROW_HALO
    return pl.pallas_call(
        functools.partial(_ffn_kernel, alpha=alpha),
        grid=(b, s // tm, nff),
        in_specs=[
            pl.BlockSpec((None, tm, d), lambda i, j, k: (i, j, 0)),
            pl.BlockSpec((None, ROW_HALO, d), lambda i, j, k: (i, jnp.maximum(j * halo_blocks - 1, 0), 0)),
            pl.BlockSpec((None, 1, d), lambda i, j, k: (i, 0, 0)),
            pl.BlockSpec((None, 1, d), lambda i, j, k: (i, 0, 0)),
            pl.BlockSpec((None, d, tn), lambda i, j, k: (layer, 0, k)),
            pl.BlockSpec((None, d, tn), lambda i, j, k: (layer, 0, nff + k)),
            pl.BlockSpec((None, FFN_CONV, tn), lambda i, j, k: (layer, 0, k)),
            pl.BlockSpec((None, FFN_CONV, tn), lambda i, j, k: (layer, 0, nff + k)),
            pl.BlockSpec((None, tn, d), lambda i, j, k: (layer, k, 0)),
            pl.BlockSpec((None, 1, d), lambda i, j, k: (i, 0, 0)),
            pl.BlockSpec((1, d), lambda i, j, k: (0, 0)),
            pl.BlockSpec((1, d), lambda i, j, k: (0, 0)),
        ],
        out_specs=pl.BlockSpec((None, tm, d), lambda i, j, k: (i, j, 0)),
        out_shape=jax.ShapeDtypeStruct((b, s, d), F32),
        scratch_shapes=[pltpu.VMEM((tm + ROW_HALO, d), BF16)],
        compiler_params=_params("parallel", "parallel", "arbitrary"),
        name="ffn_conv_gate_down_ln",
    )(x, x, sc, sh, w_up, w_up, conv_ff, conv_ff, w_down, g, ln_g, ln_b)


def kernel(x, c, w_ada, b_ada, w_in, conv_ml, pool_w, pool_scale, ig_bias, fg_bias, w_branch, w_out,
           w_up, conv_ff, w_down, ln_g, ln_b):
    b, s, d = x.shape
    depth = w_in.shape[0]
    alpha = (2.0 * depth) ** 0.25

    w_main = jnp.concatenate([w_in[:, :, :IF_COL], w_in[:, :, IF_COL + 2 * ML_HEADS:]], axis=2).astype(BF16)
    w_if = jnp.pad(w_in[:, :, IF_COL:IF_COL + 2 * ML_HEADS],
                   ((0, 0), (0, 0), (0, IF_PAD - 2 * ML_HEADS))).astype(BF16)
    pool_w, w_branch, w_out, w_up, w_down = (a.astype(BF16) for a in (pool_w, w_branch, w_out, w_up, w_down))
    pool_scale = pool_scale.reshape(depth, 1, D_BRANCH)

    mod = _mod_call(c, w_ada, b_ada)
    for l in range(depth):
        sh1, sc1, g1, sh2, sc2, g2 = (mod[l][:, None, i * d:(i + 1) * d] for i in range(6))

        main = _inproj_call(x, sc1, sh1, w_main, l, BF16, "in_proj")
        ifg = _inproj_call(x, sc1, sh1, w_if, l, F32, "in_proj_gates")
        ifg_t = jnp.swapaxes(ifg[:, :, :2 * ML_HEADS], 1, 2)
        gate_bias = jnp.concatenate([ig_bias[l], fg_bias[l]])
        bias_col = jnp.pad(gate_bias, (0, IF_PAD - 2 * ML_HEADS)).reshape(1, IF_PAD)
        bias_row = gate_bias.reshape(2 * ML_HEADS, 1)

        y_sb = _sb_call(main)
        y_ml = _mlstm_call(main, ifg, ifg_t, conv_ml[l], bias_col, bias_row)
        x = _mixer_out_call(main, pool_w, pool_scale, y_sb, y_ml, w_branch, w_out, x, g1,
                            ln_g[l, 0].reshape(1, d), ln_b[l, 0].reshape(1, d), alpha, l)
        x = _ffn_call(x, sc2, sh2, w_up, conv_ff, w_down, g2,
                      ln_g[l, 1].reshape(1, d), ln_b[l, 1].reshape(1, d), alpha, l)
    return x
```

```python
import functools

import jax
import jax.numpy as jnp
from jax import lax
from jax.experimental import pallas as pl
from jax.experimental.pallas import tpu as pltpu

F32 = jnp.float32
BF16 = jnp.bfloat16

D_BRANCH = 1024
POOL_WINDOWS = (2, 4, 8, 16)
POOL_GROUP_DIM = 256
POOL_HALO = 16
SB_HEADS = 8
SB_HEAD_DIM = 128
ML_HEADS = 4
ML_HEAD_DIM = 256
ML_CONV = 4
FFN_CONV = 3
FFN_SUB = 256
LN_EPS = 1e-5
GATE_COL = 8 * D_BRANCH
IF_COL = 8 * D_BRANCH
IF_PAD = 128
ROW_HALO = 16
CONV_HALO = 8

LOG2_E = 1.4426950408889634

SB_ZERO_BITS = 127.0

VMEM_LIMIT_BYTES = 48 * 1024 * 1024


def _params(*sem):
    return pltpu.CompilerParams(dimension_semantics=sem, vmem_limit_bytes=VMEM_LIMIT_BYTES)


def _pick(n, cands):
    for c in cands:
        if n % c == 0:
            return c
    raise ValueError(f"no tile in {cands} divides {n}")


def _dot(a, b):
    return jnp.dot(a, b, preferred_element_type=F32)


def _dot_nt(a, b):
    return lax.dot_general(a, b, (((1,), (1,)), ((), ())), preferred_element_type=F32)


def _dot_tn(a, b):
    return lax.dot_general(a, b, (((0,), (0,)), ((), ())), preferred_element_type=F32)


def _log_sigmoid_parts(z):
    return jnp.log1p(jnp.exp(-jnp.abs(z)))


def _split_bf16(a, terms):
    parts = []
    rem = a
    for _ in range(terms):
        p = rem.astype(BF16)
        parts.append(p)
        rem = rem - p.astype(F32)
    return parts


def _mod_kernel(c_ref, w_ref, b_ref, o_ref):
    c = c_ref[...]
    ca = (c * jax.nn.sigmoid(c)).astype(BF16)
    o_ref[...] = _dot(ca, w_ref[...].astype(BF16)) + b_ref[...]


def _mod_call(c, w_ada, b_ada):
    depth, d, n = w_ada.shape
    b = c.shape[0]
    tn = _pick(n, (1024, 512, 256, 128))
    return pl.pallas_call(
        _mod_kernel,
        grid=(depth, n // tn),
        in_specs=[
            pl.BlockSpec((b, d), lambda l, j: (0, 0)),
            pl.BlockSpec((None, d, tn), lambda l, j: (l, 0, j)),
            pl.BlockSpec((None, 1, tn), lambda l, j: (l, 0, j)),
        ],
        out_specs=pl.BlockSpec((None, b, tn), lambda l, j: (l, 0, j)),
        out_shape=jax.ShapeDtypeStruct((depth, b, n), F32),
        compiler_params=_params("parallel", "parallel"),
        name="adaln_mod",
    )(c, w_ada, b_ada.reshape(depth, 1, n))


def _inproj_kernel(x_ref, sc_ref, sh_ref, w_ref, wif_ref, o_ref, oif_ref, h_ref):
    @pl.when(pl.program_id(2) == 0)
    def _():
        h = (x_ref[...] * (1.0 + sc_ref[...]) + sh_ref[...]).astype(h_ref.dtype)
        h_ref[...] = h
        oif_ref[...] = _dot(h, wif_ref[...])

    o_ref[...] = _dot(h_ref[...], w_ref[...]).astype(o_ref.dtype)


def _inproj_call(x, sc, sh, w, w_if, layer):
    b, s, d = x.shape
    n = w.shape[2]
    tm = _pick(s, (1024, 512, 256))
    tn = _pick(n, (1024, 512, 256, 128))
    return pl.pallas_call(
        _inproj_kernel,
        grid=(b, s // tm, n // tn),
        in_specs=[
            pl.BlockSpec((None, tm, d), lambda i, j, k: (i, j, 0)),
            pl.BlockSpec((None, 1, d), lambda i, j, k: (i, 0, 0)),
            pl.BlockSpec((None, 1, d), lambda i, j, k: (i, 0, 0)),
            pl.BlockSpec((None, d, tn), lambda i, j, k: (layer, 0, k)),
            pl.BlockSpec((None, d, IF_PAD), lambda i, j, k: (layer, 0, 0)),
        ],
        out_specs=[pl.BlockSpec((None, tm, tn), lambda i, j, k: (i, j, k)),
                   pl.BlockSpec((None, tm, IF_PAD), lambda i, j, k: (i, j, 0))],
        out_shape=[jax.ShapeDtypeStruct((b, s, n), BF16), jax.ShapeDtypeStruct((b, s, IF_PAD), F32)],
        scratch_shapes=[pltpu.VMEM((tm, d), BF16)],
        compiler_params=_params("parallel", "parallel", "arbitrary"),
        name="in_proj",
    )(x, sc, sh, w, w_if)


def _pool_tile(a_ref, w_ref, scale_ref, prev_ref, st):
    t = a_ref.shape[0]

    @pl.when(st == 0)
    def _():
        prev_ref[...] = jnp.zeros_like(prev_ref)

    row = lax.broadcasted_iota(jnp.int32, (t, t), 0)
    col = lax.broadcasted_iota(jnp.int32, (t, t), 1)
    prow = lax.broadcasted_iota(jnp.int32, (t, POOL_HALO), 0)
    pcol = lax.broadcasted_iota(jnp.int32, (t, POOL_HALO), 1) - POOL_HALO
    t_abs = st * t + lax.broadcasted_iota(jnp.int32, (t, 1), 0)
    outs = []
    for g, win in enumerate(POOL_WINDOWS):
        cs = slice(g * POOL_GROUP_DIM, (g + 1) * POOL_GROUP_DIM)
        a = a_ref[:, cs]
        band = jnp.where((col <= row) & (col > row - win), 1.0, 0.0).astype(BF16)
        pband = jnp.where(pcol > prow - win, 1.0, 0.0).astype(BF16)
        wsum = _dot(band, a) + _dot(pband, prev_ref[:, cs])
        cnt = jnp.minimum(t_abs + 1, win).astype(F32)
        diff = (wsum / cnt - a.astype(F32)).astype(BF16)
        outs.append((_dot(diff, w_ref[g]) * scale_ref[:, cs]).astype(BF16))
    prev_ref[...] = a_ref[t - POOL_HALO:, :]
    return jnp.concatenate(outs, axis=1)


def _sb_kernel(q_ref, k_ref, v_ref, o_ref, *, tq):
    qb = pl.program_id(1)
    scale = SB_HEAD_DIM ** -0.5 * LOG2_E
    row = lax.broadcasted_iota(jnp.int32, (tq, tq), 0)
    col = lax.broadcasted_iota(jnp.int32, (tq, tq), 1)
    below = row > col
    after_sum = jnp.where(below, 1.0, 0.0).astype(BF16)

    def sweep(rows, accs, rs):
        diag = accs is None
        heads = [slice(h * SB_HEAD_DIM, (h + 1) * SB_HEAD_DIM) for h in range(SB_HEADS)]
        zs = [_dot_nt(q_ref[:, hs], k_ref[rows, hs]) * scale for hs in heads]
        log_betas, drops, his, los = [], [], [], []
        for z in zs:
            l = jnp.log2(1.0 + jnp.exp2(-jnp.abs(z)))
            drop = jnp.maximum(z, 0.0) + l
            if diag:
                drop = jnp.where(below, drop, 0.0)
            hi, lo = _split_bf16(drop, 2)
            log_betas.append(jnp.minimum(z, 0.0) - l)
            drops.append(drop)
            his.append(hi)
            los.append(lo)
        sums = _dot(jnp.concatenate(his + los, axis=0), after_sum)
        attns, new_rs, rmin = [], [], None
        for h in range(SB_HEADS):
            after = sums[h * tq:(h + 1) * tq] + sums[(SB_HEADS + h) * tq:(SB_HEADS + h + 1) * tq]
            rowsum = jnp.sum(drops[h], axis=1, keepdims=True)
            if diag:
                attn = jnp.where(below, jnp.exp2(log_betas[h] - after), 0.0)
                rn = rowsum
            else:
                attn = jnp.exp2(log_betas[h] - (after + rs[h]))
                rn = rs[h] + rowsum
            attns.append(attn.astype(BF16))
            new_rs.append(rn)
            rmin = rn if rmin is None else jnp.minimum(rmin, rn)
        pvs = [_dot(attns[h], v_ref[rows, hs]) for h, hs in enumerate(heads)]
        new_accs = pvs if diag else [a + p for a, p in zip(accs, pvs)]
        return tuple(new_accs), tuple(new_rs), jnp.min(rmin)

    accs0, rs0, rmin0 = sweep(pl.ds(pl.multiple_of(qb * tq, tq), tq), None, None)

    def cond(carry):
        kb, rmin, _, _ = carry
        return jnp.logical_and(kb >= 0, rmin < SB_ZERO_BITS)

    def body(carry):
        kb, _, accs, rs = carry
        accs, rs, rmin = sweep(pl.ds(pl.multiple_of(kb * tq, tq), tq), accs, rs)
        return kb - 1, rmin, accs, rs

    _, _, accs, _ = lax.while_loop(cond, body, (qb - 1, rmin0, accs0, rs0))
    for h in range(SB_HEADS):
        o_ref[:, h * SB_HEAD_DIM:(h + 1) * SB_HEAD_DIM] = accs[h].astype(o_ref.dtype)


def _sb_call(main):
    b, s, _ = main.shape
    tq = 128
    return pl.pallas_call(
        functools.partial(_sb_kernel, tq=tq),
        grid=(b, s // tq),
        in_specs=[
            pl.BlockSpec((None, tq, D_BRANCH), lambda i, j: (i, j, 1)),
            pl.BlockSpec((None, s, D_BRANCH), lambda i, j: (i, 0, 2)),
            pl.BlockSpec((None, s, D_BRANCH), lambda i, j: (i, 0, 3)),
        ],
        out_specs=pl.BlockSpec((None, tq, D_BRANCH), lambda i, j: (i, j, 0)),
        out_shape=jax.ShapeDtypeStruct((b, s, D_BRANCH), BF16),
        compiler_params=_params("parallel", "arbitrary"),
        name="stick_breaking",
    )(main, main, main)


def _mlstm_kernel(qk_ref, v_ref, og_ref, if_ref, ift_ref, cw_ref, bcol_ref, brow_ref, o_ref,
                  ext_ref, ct_ref, n_ref, m_ref, *, chunk):
    L = chunk

    @pl.when(pl.program_id(1) == 0)
    def _():
        ext_ref[0:CONV_HALO, :] = jnp.zeros((CONV_HALO, ext_ref.shape[1]), F32)
        ct_ref[...] = jnp.zeros_like(ct_ref)
        n_ref[...] = jnp.zeros_like(n_ref)
        m_ref[...] = jnp.zeros_like(m_ref)

    ext_ref[CONV_HALO:, :] = qk_ref[...].astype(F32)
    ext = ext_ref[...]
    cw = cw_ref[...]
    conv = ext * cw[ML_CONV - 1:ML_CONV, :]
    for j in range(1, ML_CONV):
        conv = conv + pltpu.roll(ext, j, axis=0) * cw[ML_CONV - 1 - j:ML_CONV - j, :]
    conv = conv[CONV_HALO:, :]
    qk = conv * jax.nn.sigmoid(conv)
    ext_ref[0:CONV_HALO, :] = ext_ref[L:L + CONV_HALO, :]

    if_c = if_ref[...] + bcol_ref[...]
    if_r = ift_ref[...] + brow_ref[...]
    logf_c = jnp.minimum(if_c, 0.0) - _log_sigmoid_parts(if_c)
    logf_r = jnp.minimum(if_r, 0.0) - _log_sigmoid_parts(if_r)
    row = lax.broadcasted_iota(jnp.int32, (L, L), 0)
    col = lax.broadcasted_iota(jnp.int32, (L, L), 1)
    causal = col <= row
    incl_lower = jnp.where(causal, 1.0, 0.0).astype(BF16)
    incl_upper = jnp.where(row <= col, 1.0, 0.0).astype(BF16)
    b_c = sum(_dot(incl_lower, p) for p in _split_bf16(logf_c, 3))
    b_r = sum(_dot(p, incl_upper) for p in _split_bf16(logf_r, 3))

    for h in range(ML_HEADS):
        hs = slice(h * ML_HEAD_DIM, (h + 1) * ML_HEAD_DIM)
        ks = slice(D_BRANCH + h * ML_HEAD_DIM, D_BRANCH + (h + 1) * ML_HEAD_DIM)
        bcol = b_c[:, ML_HEADS + h:ML_HEADS + h + 1]
        icol = if_c[:, h:h + 1]
        brow = b_r[ML_HEADS + h:ML_HEADS + h + 1, :]
        irow = if_r[h:h + 1, :]
        m_prev = m_ref[h:h + 1, 0:1]

        d = jnp.where(causal, bcol - brow + irow, -jnp.inf)
        inter = bcol + m_prev
        m_t = jnp.maximum(inter, jnp.max(d, axis=1, keepdims=True))
        w_intra = jnp.exp(d - m_t)
        w_inter = jnp.exp(inter - m_t)

        q = qk[:, hs].astype(BF16)
        k32 = qk[:, ks] * (ML_HEAD_DIM ** -0.5)
        v = v_ref[:, hs]
        ct = ct_ref[h]
        nrow = n_ref[h:h + 1, :]
        s = _dot_nt(q, k32.astype(BF16)) * w_intra
        num = w_inter * _dot(q, ct.astype(BF16)) + _dot(s.astype(BF16), v)
        den = (w_inter * jnp.sum(q.astype(F32) * nrow, axis=1, keepdims=True)
               + jnp.sum(s, axis=1, keepdims=True))
        hid = num / jnp.maximum(jnp.abs(den), jnp.exp(-m_t))
        og = og_ref[:, hs].astype(F32)
        o_ref[:, hs] = (hid * jax.nn.sigmoid(og)).astype(o_ref.dtype)

        b_last = bcol[L - 1:L, :]
        dec = b_last - bcol + icol
        m_new = jnp.maximum(b_last + m_prev, jnp.max(dec, axis=0, keepdims=True))
        w_s = jnp.exp(dec - m_new)
        w_prev = jnp.exp(b_last + m_prev - m_new)
        kw = k32 * w_s
        ct_ref[h] = w_prev * ct + _dot_tn(kw.astype(BF16), v)
        n_ref[h:h + 1, :] = w_prev * nrow + jnp.sum(kw, axis=0, keepdims=True)
        m_ref[h:h + 1, :] = jnp.broadcast_to(m_new, (1, m_ref.shape[1]))


def _mlstm_call(main, ifg, ifg_t, conv_w, bias_col, bias_row):
    b, s, _ = main.shape
    chunk = 256
    return pl.pallas_call(
        functools.partial(_mlstm_kernel, chunk=chunk),
        grid=(b, s // chunk),
        in_specs=[
            pl.BlockSpec((None, chunk, 2 * D_BRANCH), lambda i, j: (i, j, 2)),
            pl.BlockSpec((None, chunk, D_BRANCH), lambda i, j: (i, j, 6)),
            pl.BlockSpec((None, chunk, D_BRANCH), lambda i, j: (i, j, 7)),
            pl.BlockSpec((None, chunk, IF_PAD), lambda i, j: (i, j, 0)),
            pl.BlockSpec((None, 2 * ML_HEADS, chunk), lambda i, j: (i, 0, j)),
            pl.BlockSpec((ML_CONV, 2 * D_BRANCH), lambda i, j: (0, 0)),
            pl.BlockSpec((1, IF_PAD), lambda i, j: (0, 0)),
            pl.BlockSpec((2 * ML_HEADS, 1), lambda i, j: (0, 0)),
        ],
        out_specs=pl.BlockSpec((None, chunk, D_BRANCH), lambda i, j: (i, j, 0)),
        out_shape=jax.ShapeDtypeStruct((b, s, D_BRANCH), BF16),
        scratch_shapes=[
            pltpu.VMEM((chunk + CONV_HALO, 2 * D_BRANCH), F32),
            pltpu.VMEM((ML_HEADS, ML_HEAD_DIM, ML_HEAD_DIM), F32),
            pltpu.VMEM((2 * ML_HEADS, ML_HEAD_DIM), F32),
            pltpu.VMEM((2 * ML_HEADS, 128), F32),
        ],
        compiler_params=_params("parallel", "arbitrary"),
        name="mlstm",
    )(main, main, main, ifg, ifg_t, conv_w, bias_col, bias_row)


def _residual_ln(x, y, g, ln_g, ln_b, alpha):
    r = alpha * x + (1.0 + g) * y
    mu = jnp.mean(r, axis=-1, keepdims=True)
    cen = r - mu
    var = jnp.mean(cen * cen, axis=-1, keepdims=True)
    return cen * lax.rsqrt(var + LN_EPS) * ln_g + ln_b


def _mixer_out_kernel(ap_ref, pw_ref, ps_ref, ys_ref, ym_ref, g0_ref, g1_ref, g2_ref, wb_ref, wo_ref, x_ref,
                      g_ref, lng_ref, lnb_ref, o_ref, prev_ref, *, alpha):
    y_pool = _pool_tile(ap_ref, pw_ref, ps_ref, prev_ref, pl.program_id(1))
    acc = None
    for i, (y, gl_ref) in enumerate(((y_pool, g0_ref), (ys_ref[...], g1_ref), (ym_ref[...], g2_ref))):
        term = jax.nn.sigmoid(gl_ref[...].astype(F32)) * _dot(y, wb_ref[i])
        acc = term if acc is None else acc + term
    y = _dot(acc.astype(BF16), wo_ref[...])
    o_ref[...] = _residual_ln(x_ref[...], y, g_ref[...], lng_ref[...], lnb_ref[...], alpha)


def _resident_layer(stacked_shape, layer):
    tail = (0,) * (len(stacked_shape) - 1)
    return pl.BlockSpec((None,) + tuple(stacked_shape[1:]), lambda *_: (layer,) + tail,
                        pipeline_mode=pl.Buffered(1))


def _mixer_out_call(main, pool_w, pool_scale, y_sb, y_ml, w_branch, w_out, x, g, ln_g, ln_b, alpha, layer):
    b, s, d = x.shape
    tm = 256
    y_spec = pl.BlockSpec((None, tm, D_BRANCH), lambda i, j: (i, j, 0))

    def gate_spec(br):
        blk = GATE_COL // d + br
        return pl.BlockSpec((None, tm, d), lambda i, j: (i, j, blk))

    return pl.pallas_call(
        functools.partial(_mixer_out_kernel, alpha=alpha),
        grid=(b, s // tm),
        in_specs=[y_spec, _resident_layer(pool_w.shape, layer), _resident_layer(pool_scale.shape, layer),
                  y_spec, y_spec, gate_spec(0), gate_spec(1), gate_spec(2),
                  _resident_layer(w_branch.shape, layer), _resident_layer(w_out.shape, layer),
                  pl.BlockSpec((None, tm, d), lambda i, j: (i, j, 0)),
                  pl.BlockSpec((None, 1, d), lambda i, j: (i, 0, 0)),
                  pl.BlockSpec((1, d), lambda i, j: (0, 0)),
                  pl.BlockSpec((1, d), lambda i, j: (0, 0))],
        out_specs=pl.BlockSpec((None, tm, d), lambda i, j: (i, j, 0)),
        out_shape=jax.ShapeDtypeStruct((b, s, d), F32),
        scratch_shapes=[pltpu.VMEM((POOL_HALO, D_BRANCH), BF16)],
        compiler_params=_params("parallel", "arbitrary"),
        name="mixer_out_ln",
    )(main, pool_w, pool_scale, y_sb, y_ml, main, main, main, w_branch, w_out, x, g, ln_g, ln_b)


def _ffn_kernel(x_ref, xh_ref, sc_ref, sh_ref, wv_ref, wg_ref, cv_ref, cg_ref, wd_ref, g_ref,
                lng_ref, lnb_ref, o_ref, h_ref, *, alpha):
    n = pl.program_id(2)

    @pl.when(n == 0)
    def _():
        sc = 1.0 + sc_ref[...]
        sh = sh_ref[...]
        halo = jnp.where(pl.program_id(1) > 0, xh_ref[...] * sc + sh, 0.0)
        h_ref[0:ROW_HALO, :] = halo.astype(h_ref.dtype)
        h_ref[ROW_HALO:, :] = (x_ref[...] * sc + sh).astype(h_ref.dtype)
        o_ref[...] = jnp.zeros_like(o_ref)

    h = h_ref[...]

    def conv(up, c_ref, cs):
        c = c_ref[:, cs]
        out = up * c[FFN_CONV - 1:FFN_CONV, :]
        for j in range(1, FFN_CONV):
            out = out + pltpu.roll(up, j, axis=0) * c[FFN_CONV - 1 - j:FFN_CONV - j, :]
        return out[ROW_HALO:, :]

    subs = [slice(j * FFN_SUB, (j + 1) * FFN_SUB) for j in range(wv_ref.shape[1] // FFN_SUB)]
    ups = [(_dot(h, wv_ref[:, cs]), _dot(h, wg_ref[:, cs])) for cs in subs]
    part = None
    for cs, (up_val, up_gate) in zip(subs, ups):
        val = conv(up_val, cv_ref, cs)
        gate = conv(up_gate, cg_ref, cs)
        act = (gate * jax.nn.sigmoid(gate) * val).astype(BF16)
        down = _dot(act, wd_ref[cs, :])
        part = down if part is None else part + down
    o_ref[...] += part

    @pl.when(n == pl.num_programs(2) - 1)
    def _():
        o_ref[...] = _residual_ln(x_ref[...], o_ref[...], g_ref[...], lng_ref[...], lnb_ref[...], alpha)


def _ffn_call(x, sc, sh, w_up, conv_ff, w_down, g, ln_g, ln_b, alpha, layer):
    b, s, d = x.shape
    dff = w_down.shape[1]
    tm = _pick(s, (512, 256))
    tn = _pick(dff, (512, 256))
    nff = dff // tn
    halo_blocks = tm // ROW_HALO
    return pl.pallas_call(
        functools.partial(_ffn_kernel, alpha=alpha),
        grid=(b, s // tm, nff),
        in_specs=[
            pl.BlockSpec((None, tm, d), lambda i, j, k: (i, j, 0)),
            pl.BlockSpec((None, ROW_HALO, d), lambda i, j, k: (i, jnp.maximum(j * halo_blocks - 1, 0), 0)),
            pl.BlockSpec((None, 1, d), lambda i, j, k: (i, 0, 0)),
            pl.BlockSpec((None, 1, d), lambda i, j, k: (i, 0, 0)),
            pl.BlockSpec((None, d, tn), lambda i, j, k: (layer, 0, k)),
            pl.BlockSpec((None, d, tn), lambda i, j, k: (layer, 0, nff + k)),
            pl.BlockSpec((None, FFN_CONV, tn), lambda i, j, k: (layer, 0, k)),
            pl.BlockSpec((None, FFN_CONV, tn), lambda i, j, k: (layer, 0, nff + k)),
            pl.BlockSpec((None, tn, d), lambda i, j, k: (layer, k, 0)),
            pl.BlockSpec((None, 1, d), lambda i, j, k: (i, 0, 0)),
            pl.BlockSpec((1, d), lambda i, j, k: (0, 0)),
            pl.BlockSpec((1, d), lambda i, j, k: (0, 0)),
        ],
        out_specs=pl.BlockSpec((None, tm, d), lambda i, j, k: (i, j, 0)),
        out_shape=jax.ShapeDtypeStruct((b, s, d), F32),
        scratch_shapes=[pltpu.VMEM((tm + ROW_HALO, d), BF16)],
        compiler_params=_params("parallel", "parallel", "arbitrary"),
        name="ffn_conv_gate_down_ln",
    )(x, x, sc, sh, w_up, w_up, conv_ff, conv_ff, w_down, g, ln_g, ln_b)


def kernel(x, c, w_ada, b_ada, w_in, conv_ml, pool_w, pool_scale, ig_bias, fg_bias, w_branch, w_out,
           w_up, conv_ff, w_down, ln_g, ln_b):
    b, s, d = x.shape
    depth = w_in.shape[0]
    alpha = (2.0 * depth) ** 0.25

    w_main = jnp.concatenate([w_in[:, :, :IF_COL], w_in[:, :, IF_COL + 2 * ML_HEADS:]], axis=2).astype(BF16)
    w_if = w_in[:, :, IF_COL:IF_COL + IF_PAD].astype(BF16)
    pool_w, w_branch, w_out, w_up, w_down = (a.astype(BF16) for a in (pool_w, w_branch, w_out, w_up, w_down))
    pool_scale = pool_scale.reshape(depth, 1, D_BRANCH)

    mod = _mod_call(c, w_ada, b_ada)
    for l in range(depth):
        sh1, sc1, g1, sh2, sc2, g2 = (mod[l][:, None, i * d:(i + 1) * d] for i in range(6))

        main, ifg = _inproj_call(x, sc1, sh1, w_main, w_if, l)
        ifg_t = jnp.swapaxes(ifg[:, :, :2 * ML_HEADS], 1, 2)
        gate_bias = jnp.concatenate([ig_bias[l], fg_bias[l]])
        bias_col = jnp.pad(gate_bias, (0, IF_PAD - 2 * ML_HEADS)).reshape(1, IF_PAD)
        bias_row = gate_bias.reshape(2 * ML_HEADS, 1)

        y_sb = _sb_call(main)
        y_ml = _mlstm_call(main, ifg, ifg_t, conv_ml[l], bias_col, bias_row)
        x = _mixer_out_call(main, pool_w, pool_scale, y_sb, y_ml, w_branch, w_out, x, g1,
                            ln_g[l, 0].reshape(1, d), ln_b[l, 0].reshape(1, d), alpha, l)
        x = _ffn_call(x, sc2, sh2, w_up, conv_ff, w_down, g2,
                      ln_g[l, 1].reshape(1, d), ln_b[l, 1].reshape(1, d), alpha, l)
    return x
```

```python
import functools

import jax
import jax.numpy as jnp
from jax import lax
from jax.experimental import pallas as pl
from jax.experimental.pallas import tpu as pltpu

F32 = jnp.float32
BF16 = jnp.bfloat16

D_BRANCH = 1024
POOL_WINDOWS = (2, 4, 8, 16)
POOL_GROUP_DIM = 256
POOL_HALO = 16
SB_HEADS = 8
SB_HEAD_DIM = 128
ML_HEADS = 4
ML_HEAD_DIM = 256
ML_CONV = 4
FFN_CONV = 3
FFN_SUB = 256
LN_EPS = 1e-5
GATE_COL = 8 * D_BRANCH
IF_COL = 8 * D_BRANCH
IF_PAD = 128
ROW_HALO = 16
CONV_HALO = 8

LOG2_E = 1.4426950408889634

SB_ZERO_BITS = 127.0

VMEM_LIMIT_BYTES = 48 * 1024 * 1024
INPROJ_VMEM_LIMIT_BYTES = 56 * 1024 * 1024


def _params(*sem, vmem_limit_bytes=VMEM_LIMIT_BYTES):
    return pltpu.CompilerParams(dimension_semantics=sem, vmem_limit_bytes=vmem_limit_bytes)


def _pick(n, cands):
    for c in cands:
        if n % c == 0:
            return c
    raise ValueError(f"no tile in {cands} divides {n}")


def _dot(a, b):
    return jnp.dot(a, b, preferred_element_type=F32)


def _dot_nt(a, b):
    return lax.dot_general(a, b, (((1,), (1,)), ((), ())), preferred_element_type=F32)


def _dot_tn(a, b):
    return lax.dot_general(a, b, (((0,), (0,)), ((), ())), preferred_element_type=F32)


def _log_sigmoid_parts(z):
    return jnp.log1p(jnp.exp(-jnp.abs(z)))


def _split_bf16(a, terms):
    parts = []
    rem = a
    for _ in range(terms):
        p = rem.astype(BF16)
        parts.append(p)
        rem = rem - p.astype(F32)
    return parts


def _mod_kernel(c_ref, w_ref, b_ref, o_ref):
    c = c_ref[...]
    ca = (c * jax.nn.sigmoid(c)).astype(BF16)
    o_ref[...] = _dot(ca, w_ref[...].astype(BF16)) + b_ref[...]


def _mod_call(c, w_ada, b_ada):
    depth, d, n = w_ada.shape
    b = c.shape[0]
    tn = _pick(n, (1024, 512, 256, 128))
    return pl.pallas_call(
        _mod_kernel,
        grid=(depth, n // tn),
        in_specs=[
            pl.BlockSpec((b, d), lambda l, j: (0, 0)),
            pl.BlockSpec((None, d, tn), lambda l, j: (l, 0, j)),
            pl.BlockSpec((None, 1, tn), lambda l, j: (l, 0, j)),
        ],
        out_specs=pl.BlockSpec((None, b, tn), lambda l, j: (l, 0, j)),
        out_shape=jax.ShapeDtypeStruct((depth, b, n), F32),
        compiler_params=_params("parallel", "parallel"),
        name="adaln_mod",
    )(c, w_ada, b_ada.reshape(depth, 1, n))


def _inproj_kernel(x_ref, sc_ref, sh_ref, w_ref, wif_ref, o_ref, oif_ref, h_ref):
    @pl.when(pl.program_id(2) == 0)
    def _():
        h = (x_ref[...] * (1.0 + sc_ref[...]) + sh_ref[...]).astype(h_ref.dtype)
        h_ref[...] = h
        oif_ref[...] = _dot(h, wif_ref[...])

    o_ref[...] = _dot(h_ref[...], w_ref[...]).astype(o_ref.dtype)


def _inproj_call(x, sc, sh, w, w_if, layer):
    b, s, d = x.shape
    n = w.shape[2]
    tm = _pick(s, (1024, 512, 256))
    tn = _pick(n, (2048, 1024, 512, 256, 128))
    return pl.pallas_call(
        _inproj_kernel,
        grid=(b, s // tm, n // tn),
        in_specs=[
            pl.BlockSpec((None, tm, d), lambda i, j, k: (i, j, 0)),
            pl.BlockSpec((None, 1, d), lambda i, j, k: (i, 0, 0)),
            pl.BlockSpec((None, 1, d), lambda i, j, k: (i, 0, 0)),
            pl.BlockSpec((None, d, tn), lambda i, j, k: (layer, 0, k)),
            pl.BlockSpec((None, d, IF_PAD), lambda i, j, k: (layer, 0, 0)),
        ],
        out_specs=[pl.BlockSpec((None, tm, tn), lambda i, j, k: (i, j, k)),
                   pl.BlockSpec((None, tm, IF_PAD), lambda i, j, k: (i, j, 0))],
        out_shape=[jax.ShapeDtypeStruct((b, s, n), BF16), jax.ShapeDtypeStruct((b, s, IF_PAD), F32)],
        scratch_shapes=[pltpu.VMEM((tm, d), BF16)],
        compiler_params=_params("parallel", "parallel", "arbitrary", vmem_limit_bytes=INPROJ_VMEM_LIMIT_BYTES),
        name="in_proj",
    )(x, sc, sh, w, w_if)


def _pool_tile(a_ref, w_ref, scale_ref, prev_ref, st):
    t = a_ref.shape[0]

    @pl.when(st == 0)
    def _():
        prev_ref[...] = jnp.zeros_like(prev_ref)

    row = lax.broadcasted_iota(jnp.int32, (t, t), 0)
    col = lax.broadcasted_iota(jnp.int32, (t, t), 1)
    prow = lax.broadcasted_iota(jnp.int32, (t, POOL_HALO), 0)
    pcol = lax.broadcasted_iota(jnp.int32, (t, POOL_HALO), 1) - POOL_HALO
    t_abs = st * t + lax.broadcasted_iota(jnp.int32, (t, 1), 0)
    outs = []
    for g, win in enumerate(POOL_WINDOWS):
        cs = slice(g * POOL_GROUP_DIM, (g + 1) * POOL_GROUP_DIM)
        a = a_ref[:, cs]
        band = jnp.where((col <= row) & (col > row - win), 1.0, 0.0).astype(BF16)
        pband = jnp.where(pcol > prow - win, 1.0, 0.0).astype(BF16)
        wsum = _dot(band, a) + _dot(pband, prev_ref[:, cs])
        cnt = jnp.minimum(t_abs + 1, win).astype(F32)
        diff = (wsum / cnt - a.astype(F32)).astype(BF16)
        outs.append((_dot(diff, w_ref[g]) * scale_ref[:, cs]).astype(BF16))
    prev_ref[...] = a_ref[t - POOL_HALO:, :]
    return jnp.concatenate(outs, axis=1)


def _sb_kernel(q_ref, k_ref, v_ref, o_ref, *, tq):
    qb = pl.program_id(1)
    scale = SB_HEAD_DIM ** -0.5 * LOG2_E
    row = lax.broadcasted_iota(jnp.int32, (tq, tq), 0)
    col = lax.broadcasted_iota(jnp.int32, (tq, tq), 1)
    below = row > col
    after_sum = jnp.where(below, 1.0, 0.0).astype(BF16)

    def sweep(rows, accs, rs):
        diag = accs is None
        heads = [slice(h * SB_HEAD_DIM, (h + 1) * SB_HEAD_DIM) for h in range(SB_HEADS)]
        zs = [_dot_nt(q_ref[:, hs], k_ref[rows, hs]) * scale for hs in heads]
        log_betas, drops, his, los = [], [], [], []
        for z in zs:
            l = jnp.log2(1.0 + jnp.exp2(-jnp.abs(z)))
            drop = jnp.maximum(z, 0.0) + l
            if diag:
                drop = jnp.where(below, drop, 0.0)
            hi, lo = _split_bf16(drop, 2)
            log_betas.append(jnp.minimum(z, 0.0) - l)
            drops.append(drop)
            his.append(hi)
            los.append(lo)
        sums = _dot(jnp.concatenate(his + los, axis=0), after_sum)
        attns, new_rs, rmin = [], [], None
        for h in range(SB_HEADS):
            after = sums[h * tq:(h + 1) * tq] + sums[(SB_HEADS + h) * tq:(SB_HEADS + h + 1) * tq]
            rowsum = jnp.sum(drops[h], axis=1, keepdims=True)
            if diag:
                attn = jnp.where(below, jnp.exp2(log_betas[h] - after), 0.0)
                rn = rowsum
            else:
                attn = jnp.exp2(log_betas[h] - (after + rs[h]))
                rn = rs[h] + rowsum
            attns.append(attn.astype(BF16))
            new_rs.append(rn)
            rmin = rn if rmin is None else jnp.minimum(rmin, rn)
        pvs = [_dot(attns[h], v_ref[rows, hs]) for h, hs in enumerate(heads)]
        new_accs = pvs if diag else [a + p for a, p in zip(accs, pvs)]
        return tuple(new_accs), tuple(new_rs), jnp.min(rmin)

    accs0, rs0, rmin0 = sweep(pl.ds(pl.multiple_of(qb * tq, tq), tq), None, None)

    def cond(carry):
        kb, rmin, _, _ = carry
        return jnp.logical_and(kb >= 0, rmin < SB_ZERO_BITS)

    def body(carry):
        kb, _, accs, rs = carry
        accs, rs, rmin = sweep(pl.ds(pl.multiple_of(kb * tq, tq), tq), accs, rs)
        return kb - 1, rmin, accs, rs

    _, _, accs, _ = lax.while_loop(cond, body, (qb - 1, rmin0, accs0, rs0))
    for h in range(SB_HEADS):
        o_ref[:, h * SB_HEAD_DIM:(h + 1) * SB_HEAD_DIM] = accs[h].astype(o_ref.dtype)


def _sb_call(main):
    b, s, _ = main.shape
    tq = 128
    return pl.pallas_call(
        functools.partial(_sb_kernel, tq=tq),
        grid=(b, s // tq),
        in_specs=[
            pl.BlockSpec((None, tq, D_BRANCH), lambda i, j: (i, j, 1)),
            pl.BlockSpec((None, s, D_BRANCH), lambda i, j: (i, 0, 2)),
            pl.BlockSpec((None, s, D_BRANCH), lambda i, j: (i, 0, 3)),
        ],
        out_specs=pl.BlockSpec((None, tq, D_BRANCH), lambda i, j: (i, j, 0)),
        out_shape=jax.ShapeDtypeStruct((b, s, D_BRANCH), BF16),
        compiler_params=_params("parallel", "arbitrary"),
        name="stick_breaking",
    )(main, main, main)


def _mlstm_kernel(qk_ref, v_ref, og_ref, if_ref, ift_ref, cw_ref, bcol_ref, brow_ref, o_ref,
                  ext_ref, ct_ref, n_ref, m_ref, *, chunk):
    L = chunk

    @pl.when(pl.program_id(1) == 0)
    def _():
        ext_ref[0:CONV_HALO, :] = jnp.zeros((CONV_HALO, ext_ref.shape[1]), F32)
        ct_ref[...] = jnp.zeros_like(ct_ref)
        n_ref[...] = jnp.zeros_like(n_ref)
        m_ref[...] = jnp.zeros_like(m_ref)

    ext_ref[CONV_HALO:, :] = qk_ref[...].astype(F32)
    ext = ext_ref[...]
    cw = cw_ref[...]
    conv = ext * cw[ML_CONV - 1:ML_CONV, :]
    for j in range(1, ML_CONV):
        conv = conv + pltpu.roll(ext, j, axis=0) * cw[ML_CONV - 1 - j:ML_CONV - j, :]
    conv = conv[CONV_HALO:, :]
    qk = conv * jax.nn.sigmoid(conv)
    ext_ref[0:CONV_HALO, :] = ext_ref[L:L + CONV_HALO, :]

    if_c = if_ref[...] + bcol_ref[...]
    if_r = ift_ref[...] + brow_ref[...]
    logf_c = jnp.minimum(if_c, 0.0) - _log_sigmoid_parts(if_c)
    logf_r = jnp.minimum(if_r, 0.0) - _log_sigmoid_parts(if_r)
    row = lax.broadcasted_iota(jnp.int32, (L, L), 0)
    col = lax.broadcasted_iota(jnp.int32, (L, L), 1)
    causal = col <= row
    incl_lower = jnp.where(causal, 1.0, 0.0).astype(BF16)
    incl_upper = jnp.where(row <= col, 1.0, 0.0).astype(BF16)
    b_c = sum(_dot(incl_lower, p) for p in _split_bf16(logf_c, 3))
    b_r = sum(_dot(p, incl_upper) for p in _split_bf16(logf_r, 3))

    for h in range(ML_HEADS):
        hs = slice(h * ML_HEAD_DIM, (h + 1) * ML_HEAD_DIM)
        ks = slice(D_BRANCH + h * ML_HEAD_DIM, D_BRANCH + (h + 1) * ML_HEAD_DIM)
        bcol = b_c[:, ML_HEADS + h:ML_HEADS + h + 1]
        icol = if_c[:, h:h + 1]
        brow = b_r[ML_HEADS + h:ML_HEADS + h + 1, :]
        irow = if_r[h:h + 1, :]
        m_prev = m_ref[h:h + 1, 0:1]

        d = jnp.where(causal, bcol - brow + irow, -jnp.inf)
        inter = bcol + m_prev
        m_t = jnp.maximum(inter, jnp.max(d, axis=1, keepdims=True))
        w_intra = jnp.exp(d - m_t)
        w_inter = jnp.exp(inter - m_t)

        q = qk[:, hs].astype(BF16)
        k32 = qk[:, ks] * (ML_HEAD_DIM ** -0.5)
        v = v_ref[:, hs]
        ct = ct_ref[h]
        nrow = n_ref[h:h + 1, :]
        s = _dot_nt(q, k32.astype(BF16)) * w_intra
        num = w_inter * _dot(q, ct.astype(BF16)) + _dot(s.astype(BF16), v)
        den = (w_inter * jnp.sum(q.astype(F32) * nrow, axis=1, keepdims=True)
               + jnp.sum(s, axis=1, keepdims=True))
        hid = num / jnp.maximum(jnp.abs(den), jnp.exp(-m_t))
        og = og_ref[:, hs].astype(F32)
        o_ref[:, hs] = (hid * jax.nn.sigmoid(og)).astype(o_ref.dtype)

        b_last = bcol[L - 1:L, :]
        dec = b_last - bcol + icol
        m_new = jnp.maximum(b_last + m_prev, jnp.max(dec, axis=0, keepdims=True))
        w_s = jnp.exp(dec - m_new)
        w_prev = jnp.exp(b_last + m_prev - m_new)
        kw = k32 * w_s
        ct_ref[h] = w_prev * ct + _dot_tn(kw.astype(BF16), v)
        n_ref[h:h + 1, :] = w_prev * nrow + jnp.sum(kw, axis=0, keepdims=True)
        m_ref[h:h + 1, :] = jnp.broadcast_to(m_new, (1, m_ref.shape[1]))


def _mlstm_call(main, ifg, ifg_t, conv_w, bias_col, bias_row):
    b, s, _ = main.shape
    chunk = 256
    return pl.pallas_call(
        functools.partial(_mlstm_kernel, chunk=chunk),
        grid=(b, s // chunk),
        in_specs=[
            pl.BlockSpec((None, chunk, 2 * D_BRANCH), lambda i, j: (i, j, 2)),
            pl.BlockSpec((None, chunk, D_BRANCH), lambda i, j: (i, j, 6)),
            pl.BlockSpec((None, chunk, D_BRANCH), lambda i, j: (i, j, 7)),
            pl.BlockSpec((None, chunk, IF_PAD), lambda i, j: (i, j, 0)),
            pl.BlockSpec((None, 2 * ML_HEADS, chunk), lambda i, j: (i, 0, j)),
            pl.BlockSpec((ML_CONV, 2 * D_BRANCH), lambda i, j: (0, 0)),
            pl.BlockSpec((1, IF_PAD), lambda i, j: (0, 0)),
            pl.BlockSpec((2 * ML_HEADS, 1), lambda i, j: (0, 0)),
        ],
        out_specs=pl.BlockSpec((None, chunk, D_BRANCH), lambda i, j: (i, j, 0)),
        out_shape=jax.ShapeDtypeStruct((b, s, D_BRANCH), BF16),
        scratch_shapes=[
            pltpu.VMEM((chunk + CONV_HALO, 2 * D_BRANCH), F32),
            pltpu.VMEM((ML_HEADS, ML_HEAD_DIM, ML_HEAD_DIM), F32),
            pltpu.VMEM((2 * ML_HEADS, ML_HEAD_DIM), F32),
            pltpu.VMEM((2 * ML_HEADS, 128), F32),
        ],
        compiler_params=_params("parallel", "arbitrary"),
        name="mlstm",
    )(main, main, main, ifg, ifg_t, conv_w, bias_col, bias_row)


def _residual_ln(x, y, g, ln_g, ln_b, alpha):
    r = alpha * x + (1.0 + g) * y
    mu = jnp.mean(r, axis=-1, keepdims=True)
    cen = r - mu
    var = jnp.mean(cen * cen, axis=-1, keepdims=True)
    return cen * lax.rsqrt(var + LN_EPS) * ln_g + ln_b


def _mixer_out_kernel(ap_ref, pw_ref, ps_ref, ys_ref, ym_ref, g0_ref, g1_ref, g2_ref, wb_ref, wo_ref, x_ref,
                      g_ref, lng_ref, lnb_ref, o_ref, prev_ref, *, alpha):
    y_pool = _pool_tile(ap_ref, pw_ref, ps_ref, prev_ref, pl.program_id(1))
    acc = None
    for i, (y, gl_ref) in enumerate(((y_pool, g0_ref), (ys_ref[...], g1_ref), (ym_ref[...], g2_ref))):
        term = jax.nn.sigmoid(gl_ref[...].astype(F32)) * _dot(y, wb_ref[i])
        acc = term if acc is None else acc + term
    y = _dot(acc.astype(BF16), wo_ref[...])
    o_ref[...] = _residual_ln(x_ref[...], y, g_ref[...], lng_ref[...], lnb_ref[...], alpha)


def _resident_layer(stacked_shape, layer):
    tail = (0,) * (len(stacked_shape) - 1)
    return pl.BlockSpec((None,) + tuple(stacked_shape[1:]), lambda *_: (layer,) + tail,
                        pipeline_mode=pl.Buffered(1))


def _mixer_out_call(main, pool_w, pool_scale, y_sb, y_ml, w_branch, w_out, x, g, ln_g, ln_b, alpha, layer):
    b, s, d = x.shape
    tm = 256
    y_spec = pl.BlockSpec((None, tm, D_BRANCH), lambda i, j: (i, j, 0))

    def gate_spec(br):
        blk = GATE_COL // d + br
        return pl.BlockSpec((None, tm, d), lambda i, j: (i, j, blk))

    return pl.pallas_call(
        functools.partial(_mixer_out_kernel, alpha=alpha),
        grid=(b, s // tm),
        in_specs=[y_spec, _resident_layer(pool_w.shape, layer), _resident_layer(pool_scale.shape, layer),
                  y_spec, y_spec, gate_spec(0), gate_spec(1), gate_spec(2),
                  _resident_layer(w_branch.shape, layer), _resident_layer(w_out.shape, layer),
                  pl.BlockSpec((None, tm, d), lambda i, j: (i, j, 0)),
                  pl.BlockSpec((None, 1, d), lambda i, j: (i, 0, 0)),
                  pl.BlockSpec((1, d), lambda i, j: (0, 0)),
                  pl.BlockSpec((1, d), lambda i, j: (0, 0))],
        out_specs=pl.BlockSpec((None, tm, d), lambda i, j: (i, j, 0)),
        out_shape=jax.ShapeDtypeStruct((b, s, d), F32),
        scratch_shapes=[pltpu.VMEM((POOL_HALO, D_BRANCH), BF16)],
        compiler_params=_params("parallel", "arbitrary"),
        name="mixer_out_ln",
    )(main, pool_w, pool_scale, y_sb, y_ml, main, main, main, w_branch, w_out, x, g, ln_g, ln_b)


def _ffn_kernel(x_ref, xh_ref, sc_ref, sh_ref, wv_ref, wg_ref, cv_ref, cg_ref, wd_ref, g_ref,
                lng_ref, lnb_ref, o_ref, h_ref, *, alpha):
    n = pl.program_id(2)

    @pl.when(n == 0)
    def _():
        sc = 1.0 + sc_ref[...]
        sh = sh_ref[...]
        halo = jnp.where(pl.program_id(1) > 0, xh_ref[...] * sc + sh, 0.0)
        h_ref[0:ROW_HALO, :] = halo.astype(h_ref.dtype)
        h_ref[ROW_HALO:, :] = (x_ref[...] * sc + sh).astype(h_ref.dtype)
        o_ref[...] = jnp.zeros_like(o_ref)

    h = h_ref[...]

    def conv(up, c_ref, cs):
        c = c_ref[:, cs]
        out = up * c[FFN_CONV - 1:FFN_CONV, :]
        for j in range(1, FFN_CONV):
            out = out + pltpu.roll(up, j, axis=0) * c[FFN_CONV - 1 - j:FFN_CONV - j, :]
        return out[ROW_HALO:, :]

    subs = [slice(j * FFN_SUB, (j + 1) * FFN_SUB) for j in range(wv_ref.shape[1] // FFN_SUB)]
    ups = [(_dot(h, wv_ref[:, cs]), _dot(h, wg_ref[:, cs])) for cs in subs]
    part = None
    for cs, (up_val, up_gate) in zip(subs, ups):
        val = conv(up_val, cv_ref, cs)
        gate = conv(up_gate, cg_ref, cs)
        act = (gate * jax.nn.sigmoid(gate) * val).astype(BF16)
        down = _dot(act, wd_ref[cs, :])
        part = down if part is None else part + down
    o_ref[...] += part

    @pl.when(n == pl.num_programs(2) - 1)
    def _():
        o_ref[...] = _residual_ln(x_ref[...], o_ref[...], g_ref[...], lng_ref[...], lnb_ref[...], alpha)


def _ffn_call(x, sc, sh, w_up, conv_ff, w_down, g, ln_g, ln_b, alpha, layer):
    b, s, d = x.shape
    dff = w_down.shape[1]
    tm = _pick(s, (512, 256))
    tn = _pick(dff, (512, 256))
    nff = dff // tn
    halo_blocks = tm // ROW_HALO
    return pl.pallas_call(
        functools.partial(_ffn_kernel, alpha=alpha),
        grid=(b, s // tm, nff),
        in_specs=[
            pl.BlockSpec((None, tm, d), lambda i, j, k: (i, j, 0)),
            pl.BlockSpec((None, ROW_HALO, d), lambda i, j, k: (i, jnp.maximum(j * halo_blocks - 1, 0), 0)),
            pl.BlockSpec((None, 1, d), lambda i, j, k: (i, 0, 0)),
            pl.BlockSpec((None, 1, d), lambda i, j, k: (i, 0, 0)),
            pl.BlockSpec((None, d, tn), lambda i, j, k: (layer, 0, k)),
            pl.BlockSpec((None, d, tn), lambda i, j, k: (layer, 0, nff + k)),
            pl.BlockSpec((None, FFN_CONV, tn), lambda i, j, k: (layer, 0, k)),
            pl.BlockSpec((None, FFN_CONV, tn), lambda i, j, k: (layer, 0, nff + k)),
            pl.BlockSpec((None, tn, d), lambda i, j, k: (layer, k, 0)),
            pl.BlockSpec((None, 1, d), lambda i, j, k: (i, 0, 0)),
            pl.BlockSpec((1, d), lambda i, j, k: (0, 0)),
            pl.BlockSpec((1, d), lambda i, j, k: (0, 0)),
        ],
        out_specs=pl.BlockSpec((None, tm, d), lambda i, j, k: (i, j, 0)),
        out_shape=jax.ShapeDtypeStruct((b, s, d), F32),
        scratch_shapes=[pltpu.VMEM((tm + ROW_HALO, d), BF16)],
        compiler_params=_params("parallel", "parallel", "arbitrary"),
        name="ffn_conv_gate_down_ln",
    )(x, x, sc, sh, w_up, w_up, conv_ff, conv_ff, w_down, g, ln_g, ln_b)


def kernel(x, c, w_ada, b_ada, w_in, conv_ml, pool_w, pool_scale, ig_bias, fg_bias, w_branch, w_out,
           w_up, conv_ff, w_down, ln_g, ln_b):
    b, s, d = x.shape
    depth = w_in.shape[0]
    alpha = (2.0 * depth) ** 0.25

    w_main = jnp.concatenate([w_in[:, :, :IF_COL], w_in[:, :, IF_COL + 2 * ML_HEADS:]], axis=2).astype(BF16)
    w_if = w_in[:, :, IF_COL:IF_COL + IF_PAD].astype(BF16)
    pool_w, w_branch, w_out, w_up, w_down = (a.astype(BF16) for a in (pool_w, w_branch, w_out, w_up, w_down))
    pool_scale = pool_scale.reshape(depth, 1, D_BRANCH)

    mod = _mod_call(c, w_ada, b_ada)
    for l in range(depth):
        sh1, sc1, g1, sh2, sc2, g2 = (mod[l][:, None, i * d:(i + 1) * d] for i in range(6))

        main, ifg = _inproj_call(x, sc1, sh1, w_main, w_if, l)
        ifg_t = jnp.swapaxes(ifg[:, :, :2 * ML_HEADS], 1, 2)
        gate_bias = jnp.concatenate([ig_bias[l], fg_bias[l]])
        bias_col = jnp.pad(gate_bias, (0, IF_PAD - 2 * ML_HEADS)).reshape(1, IF_PAD)
        bias_row = gate_bias.reshape(2 * ML_HEADS, 1)

        y_sb = _sb_call(main)
        y_ml = _mlstm_call(main, ifg, ifg_t, conv_ml[l], bias_col, bias_row)
        x = _mixer_out_call(main, pool_w, pool_scale, y_sb, y_ml, w_branch, w_out, x, g1,
                            ln_g[l, 0].reshape(1, d), ln_b[l, 0].reshape(1, d), alpha, l)
        x = _ffn_call(x, sc2, sh2, w_up, conv_ff, w_down, g2,
                      ln_g[l, 1].reshape(1, d), ln_b[l, 1].reshape(1, d), alpha, l)
    return x
```

```python
import functools

import jax
import jax.numpy as jnp
from jax import lax
from jax.experimental import pallas as pl
from jax.experimental.pallas import tpu as pltpu

F32 = jnp.float32
BF16 = jnp.bfloat16

D_BRANCH = 1024
POOL_WINDOWS = (2, 4, 8, 16)
POOL_GROUP_DIM = 256
POOL_HALO = 16
SB_HEADS = 8
SB_HEAD_DIM = 128
ML_HEADS = 4
ML_HEAD_DIM = 256
ML_CONV = 4
FFN_CONV = 3
FFN_SUB = 256
LN_EPS = 1e-5
GATE_COL = 8 * D_BRANCH
IF_COL = 8 * D_BRANCH
IF_PAD = 128
ROW_HALO = 16
CONV_HALO = 8

LOG2_E = 1.4426950408889634

SB_ZERO_BITS = 127.0

VMEM_LIMIT_BYTES = 48 * 1024 * 1024
INPROJ_VMEM_LIMIT_BYTES = 56 * 1024 * 1024
FFN_VMEM_LIMIT_BYTES = 58 * 1024 * 1024


def _params(*sem, vmem_limit_bytes=VMEM_LIMIT_BYTES):
    return pltpu.CompilerParams(dimension_semantics=sem, vmem_limit_bytes=vmem_limit_bytes)


def _pick(n, cands):
    for c in cands:
        if n % c == 0:
            return c
    raise ValueError(f"no tile in {cands} divides {n}")


def _dot(a, b):
    return jnp.dot(a, b, preferred_element_type=F32)


def _dot_nt(a, b):
    return lax.dot_general(a, b, (((1,), (1,)), ((), ())), preferred_element_type=F32)


def _dot_tn(a, b):
    return lax.dot_general(a, b, (((0,), (0,)), ((), ())), preferred_element_type=F32)


def _log_sigmoid_parts(z):
    return jnp.log1p(jnp.exp(-jnp.abs(z)))


def _split_bf16(a, terms):
    parts = []
    rem = a
    for _ in range(terms):
        p = rem.astype(BF16)
        parts.append(p)
        rem = rem - p.astype(F32)
    return parts


def _mod_kernel(c_ref, w_ref, b_ref, o_ref):
    c = c_ref[...]
    ca = (c * jax.nn.sigmoid(c)).astype(BF16)
    o_ref[...] = _dot(ca, w_ref[...].astype(BF16)) + b_ref[...]


def _mod_call(c, w_ada, b_ada):
    depth, d, n = w_ada.shape
    b = c.shape[0]
    tn = _pick(n, (1024, 512, 256, 128))
    return pl.pallas_call(
        _mod_kernel,
        grid=(depth, n // tn),
        in_specs=[
            pl.BlockSpec((b, d), lambda l, j: (0, 0)),
            pl.BlockSpec((None, d, tn), lambda l, j: (l, 0, j)),
            pl.BlockSpec((None, 1, tn), lambda l, j: (l, 0, j)),
        ],
        out_specs=pl.BlockSpec((None, b, tn), lambda l, j: (l, 0, j)),
        out_shape=jax.ShapeDtypeStruct((depth, b, n), F32),
        compiler_params=_params("parallel", "parallel"),
        name="adaln_mod",
    )(c, w_ada, b_ada.reshape(depth, 1, n))


def _inproj_kernel(x_ref, sc_ref, sh_ref, w_ref, wif_ref, o_ref, oif_ref, h_ref):
    @pl.when(pl.program_id(2) == 0)
    def _():
        h = (x_ref[...] * (1.0 + sc_ref[...]) + sh_ref[...]).astype(h_ref.dtype)
        h_ref[...] = h
        oif_ref[...] = _dot(h, wif_ref[...])

    o_ref[...] = _dot(h_ref[...], w_ref[...]).astype(o_ref.dtype)


def _inproj_call(x, sc, sh, w, w_if, layer):
    b, s, d = x.shape
    n = w.shape[2]
    tm = _pick(s, (1024, 512, 256))
    tn = _pick(n, (2048, 1024, 512, 256, 128))
    return pl.pallas_call(
        _inproj_kernel,
        grid=(b, s // tm, n // tn),
        in_specs=[
            pl.BlockSpec((None, tm, d), lambda i, j, k: (i, j, 0)),
            pl.BlockSpec((None, 1, d), lambda i, j, k: (i, 0, 0)),
            pl.BlockSpec((None, 1, d), lambda i, j, k: (i, 0, 0)),
            pl.BlockSpec((None, d, tn), lambda i, j, k: (layer, 0, k)),
            pl.BlockSpec((None, d, IF_PAD), lambda i, j, k: (layer, 0, 0)),
        ],
        out_specs=[pl.BlockSpec((None, tm, tn), lambda i, j, k: (i, j, k)),
                   pl.BlockSpec((None, tm, IF_PAD), lambda i, j, k: (i, j, 0))],
        out_shape=[jax.ShapeDtypeStruct((b, s, n), BF16), jax.ShapeDtypeStruct((b, s, IF_PAD), F32)],
        scratch_shapes=[pltpu.VMEM((tm, d), BF16)],
        compiler_params=_params("parallel", "parallel", "arbitrary", vmem_limit_bytes=INPROJ_VMEM_LIMIT_BYTES),
        name="in_proj",
    )(x, sc, sh, w, w_if)


def _pool_tile(a_ref, w_ref, scale_ref, prev_ref, st):
    t = a_ref.shape[0]

    @pl.when(st == 0)
    def _():
        prev_ref[...] = jnp.zeros_like(prev_ref)

    row = lax.broadcasted_iota(jnp.int32, (t, t), 0)
    col = lax.broadcasted_iota(jnp.int32, (t, t), 1)
    prow = lax.broadcasted_iota(jnp.int32, (t, POOL_HALO), 0)
    pcol = lax.broadcasted_iota(jnp.int32, (t, POOL_HALO), 1) - POOL_HALO
    t_abs = st * t + lax.broadcasted_iota(jnp.int32, (t, 1), 0)
    outs = []
    for g, win in enumerate(POOL_WINDOWS):
        cs = slice(g * POOL_GROUP_DIM, (g + 1) * POOL_GROUP_DIM)
        a = a_ref[:, cs]
        band = jnp.where((col <= row) & (col > row - win), 1.0, 0.0).astype(BF16)
        pband = jnp.where(pcol > prow - win, 1.0, 0.0).astype(BF16)
        wsum = _dot(band, a) + _dot(pband, prev_ref[:, cs])
        cnt = jnp.minimum(t_abs + 1, win).astype(F32)
        diff = (wsum / cnt - a.astype(F32)).astype(BF16)
        outs.append((_dot(diff, w_ref[g]) * scale_ref[:, cs]).astype(BF16))
    prev_ref[...] = a_ref[t - POOL_HALO:, :]
    return jnp.concatenate(outs, axis=1)


def _sb_kernel(q_ref, k_ref, v_ref, o_ref, *, tq):
    qb = pl.program_id(1)
    scale = SB_HEAD_DIM ** -0.5 * LOG2_E
    row = lax.broadcasted_iota(jnp.int32, (tq, tq), 0)
    col = lax.broadcasted_iota(jnp.int32, (tq, tq), 1)
    below = row > col
    after_sum = jnp.where(below, 1.0, 0.0).astype(BF16)

    def sweep(rows, accs, rs):
        diag = accs is None
        heads = [slice(h * SB_HEAD_DIM, (h + 1) * SB_HEAD_DIM) for h in range(SB_HEADS)]
        zs = [_dot_nt(q_ref[:, hs], k_ref[rows, hs]) * scale for hs in heads]
        log_betas, drops, his, los = [], [], [], []
        for z in zs:
            l = jnp.log2(1.0 + jnp.exp2(-jnp.abs(z)))
            drop = jnp.maximum(z, 0.0) + l
            if diag:
                drop = jnp.where(below, drop, 0.0)
            hi, lo = _split_bf16(drop, 2)
            log_betas.append(jnp.minimum(z, 0.0) - l)
            drops.append(drop)
            his.append(hi)
            los.append(lo)
        sums = _dot(jnp.concatenate(his + los, axis=0), after_sum)
        attns, new_rs, rmin = [], [], None
        for h in range(SB_HEADS):
            after = sums[h * tq:(h + 1) * tq] + sums[(SB_HEADS + h) * tq:(SB_HEADS + h + 1) * tq]
            rowsum = jnp.sum(drops[h], axis=1, keepdims=True)
            if diag:
                attn = jnp.where(below, jnp.exp2(log_betas[h] - after), 0.0)
                rn = rowsum
            else:
                attn = jnp.exp2(log_betas[h] - (after + rs[h]))
                rn = rs[h] + rowsum
            attns.append(attn.astype(BF16))
            new_rs.append(rn)
            rmin = rn if rmin is None else jnp.minimum(rmin, rn)
        pvs = [_dot(attns[h], v_ref[rows, hs]) for h, hs in enumerate(heads)]
        new_accs = pvs if diag else [a + p for a, p in zip(accs, pvs)]
        return tuple(new_accs), tuple(new_rs), jnp.min(rmin)

    accs0, rs0, rmin0 = sweep(pl.ds(pl.multiple_of(qb * tq, tq), tq), None, None)

    def cond(carry):
        kb, rmin, _, _ = carry
        return jnp.logical_and(kb >= 0, rmin < SB_ZERO_BITS)

    def body(carry):
        kb, _, accs, rs = carry
        accs, rs, rmin = sweep(pl.ds(pl.multiple_of(kb * tq, tq), tq), accs, rs)
        return kb - 1, rmin, accs, rs

    _, _, accs, _ = lax.while_loop(cond, body, (qb - 1, rmin0, accs0, rs0))
    for h in range(SB_HEADS):
        o_ref[:, h * SB_HEAD_DIM:(h + 1) * SB_HEAD_DIM] = accs[h].astype(o_ref.dtype)


def _sb_call(main):
    b, s, _ = main.shape
    tq = 128
    return pl.pallas_call(
        functools.partial(_sb_kernel, tq=tq),
        grid=(b, s // tq),
        in_specs=[
            pl.BlockSpec((None, tq, D_BRANCH), lambda i, j: (i, j, 1)),
            pl.BlockSpec((None, s, D_BRANCH), lambda i, j: (i, 0, 2)),
            pl.BlockSpec((None, s, D_BRANCH), lambda i, j: (i, 0, 3)),
        ],
        out_specs=pl.BlockSpec((None, tq, D_BRANCH), lambda i, j: (i, j, 0)),
        out_shape=jax.ShapeDtypeStruct((b, s, D_BRANCH), BF16),
        compiler_params=_params("parallel", "arbitrary"),
        name="stick_breaking",
    )(main, main, main)


def _mlstm_kernel(qk_ref, v_ref, og_ref, if_ref, ift_ref, cw_ref, bcol_ref, brow_ref, o_ref,
                  ext_ref, ct_ref, n_ref, m_ref, *, chunk):
    L = chunk

    @pl.when(pl.program_id(1) == 0)
    def _():
        ext_ref[0:CONV_HALO, :] = jnp.zeros((CONV_HALO, ext_ref.shape[1]), F32)
        ct_ref[...] = jnp.zeros_like(ct_ref)
        n_ref[...] = jnp.zeros_like(n_ref)
        m_ref[...] = jnp.zeros_like(m_ref)

    ext_ref[CONV_HALO:, :] = qk_ref[...].astype(F32)
    ext = ext_ref[...]
    cw = cw_ref[...]
    conv = ext * cw[ML_CONV - 1:ML_CONV, :]
    for j in range(1, ML_CONV):
        conv = conv + pltpu.roll(ext, j, axis=0) * cw[ML_CONV - 1 - j:ML_CONV - j, :]
    conv = conv[CONV_HALO:, :]
    qk = conv * jax.nn.sigmoid(conv)
    ext_ref[0:CONV_HALO, :] = ext_ref[L:L + CONV_HALO, :]

    if_c = if_ref[...] + bcol_ref[...]
    if_r = ift_ref[...] + brow_ref[...]
    logf_c = jnp.minimum(if_c, 0.0) - _log_sigmoid_parts(if_c)
    logf_r = jnp.minimum(if_r, 0.0) - _log_sigmoid_parts(if_r)
    row = lax.broadcasted_iota(jnp.int32, (L, L), 0)
    col = lax.broadcasted_iota(jnp.int32, (L, L), 1)
    causal = col <= row
    incl_lower = jnp.where(causal, 1.0, 0.0).astype(BF16)
    incl_upper = jnp.where(row <= col, 1.0, 0.0).astype(BF16)
    b_c = sum(_dot(incl_lower, p) for p in _split_bf16(logf_c, 3))
    b_r = sum(_dot(p, incl_upper) for p in _split_bf16(logf_r, 3))

    for h in range(ML_HEADS):
        hs = slice(h * ML_HEAD_DIM, (h + 1) * ML_HEAD_DIM)
        ks = slice(D_BRANCH + h * ML_HEAD_DIM, D_BRANCH + (h + 1) * ML_HEAD_DIM)
        bcol = b_c[:, ML_HEADS + h:ML_HEADS + h + 1]
        icol = if_c[:, h:h + 1]
        brow = b_r[ML_HEADS + h:ML_HEADS + h + 1, :]
        irow = if_r[h:h + 1, :]
        m_prev = m_ref[h:h + 1, 0:1]

        d = jnp.where(causal, bcol - brow + irow, -jnp.inf)
        inter = bcol + m_prev
        m_t = jnp.maximum(inter, jnp.max(d, axis=1, keepdims=True))
        w_intra = jnp.exp(d - m_t)
        w_inter = jnp.exp(inter - m_t)

        q = qk[:, hs].astype(BF16)
        k32 = qk[:, ks] * (ML_HEAD_DIM ** -0.5)
        v = v_ref[:, hs]
        ct = ct_ref[h]
        nrow = n_ref[h:h + 1, :]
        s = _dot_nt(q, k32.astype(BF16)) * w_intra
        num = w_inter * _dot(q, ct.astype(BF16)) + _dot(s.astype(BF16), v)
        den = (w_inter * jnp.sum(q.astype(F32) * nrow, axis=1, keepdims=True)
               + jnp.sum(s, axis=1, keepdims=True))
        hid = num / jnp.maximum(jnp.abs(den), jnp.exp(-m_t))
        og = og_ref[:, hs].astype(F32)
        o_ref[:, hs] = (hid * jax.nn.sigmoid(og)).astype(o_ref.dtype)

        b_last = bcol[L - 1:L, :]
        dec = b_last - bcol + icol
        m_new = jnp.maximum(b_last + m_prev, jnp.max(dec, axis=0, keepdims=True))
        w_s = jnp.exp(dec - m_new)
        w_prev = jnp.exp(b_last + m_prev - m_new)
        kw = k32 * w_s
        ct_ref[h] = w_prev * ct + _dot_tn(kw.astype(BF16), v)
        n_ref[h:h + 1, :] = w_prev * nrow + jnp.sum(kw, axis=0, keepdims=True)
        m_ref[h:h + 1, :] = jnp.broadcast_to(m_new, (1, m_ref.shape[1]))


def _mlstm_call(main, ifg, ifg_t, conv_w, bias_col, bias_row):
    b, s, _ = main.shape
    chunk = 256
    return pl.pallas_call(
        functools.partial(_mlstm_kernel, chunk=chunk),
        grid=(b, s // chunk),
        in_specs=[
            pl.BlockSpec((None, chunk, 2 * D_BRANCH), lambda i, j: (i, j, 2)),
            pl.BlockSpec((None, chunk, D_BRANCH), lambda i, j: (i, j, 6)),
            pl.BlockSpec((None, chunk, D_BRANCH), lambda i, j: (i, j, 7)),
            pl.BlockSpec((None, chunk, IF_PAD), lambda i, j: (i, j, 0)),
            pl.BlockSpec((None, 2 * ML_HEADS, chunk), lambda i, j: (i, 0, j)),
            pl.BlockSpec((ML_CONV, 2 * D_BRANCH), lambda i, j: (0, 0)),
            pl.BlockSpec((1, IF_PAD), lambda i, j: (0, 0)),
            pl.BlockSpec((2 * ML_HEADS, 1), lambda i, j: (0, 0)),
        ],
        out_specs=pl.BlockSpec((None, chunk, D_BRANCH), lambda i, j: (i, j, 0)),
        out_shape=jax.ShapeDtypeStruct((b, s, D_BRANCH), BF16),
        scratch_shapes=[
            pltpu.VMEM((chunk + CONV_HALO, 2 * D_BRANCH), F32),
            pltpu.VMEM((ML_HEADS, ML_HEAD_DIM, ML_HEAD_DIM), F32),
            pltpu.VMEM((2 * ML_HEADS, ML_HEAD_DIM), F32),
            pltpu.VMEM((2 * ML_HEADS, 128), F32),
        ],
        compiler_params=_params("parallel", "arbitrary"),
        name="mlstm",
    )(main, main, main, ifg, ifg_t, conv_w, bias_col, bias_row)


def _residual_ln(x, y, g, ln_g, ln_b, alpha):
    r = alpha * x + (1.0 + g) * y
    mu = jnp.mean(r, axis=-1, keepdims=True)
    cen = r - mu
    var = jnp.mean(cen * cen, axis=-1, keepdims=True)
    return cen * lax.rsqrt(var + LN_EPS) * ln_g + ln_b


def _mixer_out_kernel(ap_ref, pw_ref, ps_ref, ys_ref, ym_ref, g0_ref, g1_ref, g2_ref, wb_ref, wo_ref, x_ref,
                      g_ref, lng_ref, lnb_ref, o_ref, prev_ref, *, alpha):
    y_pool = _pool_tile(ap_ref, pw_ref, ps_ref, prev_ref, pl.program_id(1))
    acc = None
    for i, (y, gl_ref) in enumerate(((y_pool, g0_ref), (ys_ref[...], g1_ref), (ym_ref[...], g2_ref))):
        term = jax.nn.sigmoid(gl_ref[...].astype(F32)) * _dot(y, wb_ref[i])
        acc = term if acc is None else acc + term
    y = _dot(acc.astype(BF16), wo_ref[...])
    o_ref[...] = _residual_ln(x_ref[...], y, g_ref[...], lng_ref[...], lnb_ref[...], alpha)


def _resident_layer(stacked_shape, layer):
    tail = (0,) * (len(stacked_shape) - 1)
    return pl.BlockSpec((None,) + tuple(stacked_shape[1:]), lambda *_: (layer,) + tail,
                        pipeline_mode=pl.Buffered(1))


def _mixer_out_call(main, pool_w, pool_scale, y_sb, y_ml, w_branch, w_out, x, g, ln_g, ln_b, alpha, layer):
    b, s, d = x.shape
    tm = 256
    y_spec = pl.BlockSpec((None, tm, D_BRANCH), lambda i, j: (i, j, 0))

    def gate_spec(br):
        blk = GATE_COL // d + br
        return pl.BlockSpec((None, tm, d), lambda i, j: (i, j, blk))

    return pl.pallas_call(
        functools.partial(_mixer_out_kernel, alpha=alpha),
        grid=(b, s // tm),
        in_specs=[y_spec, _resident_layer(pool_w.shape, layer), _resident_layer(pool_scale.shape, layer),
                  y_spec, y_spec, gate_spec(0), gate_spec(1), gate_spec(2),
                  _resident_layer(w_branch.shape, layer), _resident_layer(w_out.shape, layer),
                  pl.BlockSpec((None, tm, d), lambda i, j: (i, j, 0)),
                  pl.BlockSpec((None, 1, d), lambda i, j: (i, 0, 0)),
                  pl.BlockSpec((1, d), lambda i, j: (0, 0)),
                  pl.BlockSpec((1, d), lambda i, j: (0, 0))],
        out_specs=pl.BlockSpec((None, tm, d), lambda i, j: (i, j, 0)),
        out_shape=jax.ShapeDtypeStruct((b, s, d), F32),
        scratch_shapes=[pltpu.VMEM((POOL_HALO, D_BRANCH), BF16)],
        compiler_params=_params("parallel", "arbitrary"),
        name="mixer_out_ln",
    )(main, pool_w, pool_scale, y_sb, y_ml, main, main, main, w_branch, w_out, x, g, ln_g, ln_b)


def _ffn_kernel(x_ref, xh_ref, sc_ref, sh_ref, wv_ref, wg_ref, cv_ref, cg_ref, wd_ref, g_ref,
                lng_ref, lnb_ref, o_ref, h_ref, *, alpha):
    n = pl.program_id(2)

    @pl.when(n == 0)
    def _():
        sc = 1.0 + sc_ref[...]
        sh = sh_ref[...]
        halo = jnp.where(pl.program_id(1) > 0, xh_ref[...] * sc + sh, 0.0)
        h_ref[0:ROW_HALO, :] = halo.astype(h_ref.dtype)
        h_ref[ROW_HALO:, :] = (x_ref[...] * sc + sh).astype(h_ref.dtype)
        o_ref[...] = jnp.zeros_like(o_ref)

    h = h_ref[...]

    def conv(up, c_ref, cs):
        c = c_ref[:, cs]
        out = up * c[FFN_CONV - 1:FFN_CONV, :]
        for j in range(1, FFN_CONV):
            out = out + pltpu.roll(up, j, axis=0) * c[FFN_CONV - 1 - j:FFN_CONV - j, :]
        return out[ROW_HALO:, :]

    subs = [slice(j * FFN_SUB, (j + 1) * FFN_SUB) for j in range(wv_ref.shape[1] // FFN_SUB)]
    ups = [(_dot(h, wv_ref[:, cs]), _dot(h, wg_ref[:, cs])) for cs in subs]
    part = None
    for cs, (up_val, up_gate) in zip(subs, ups):
        val = conv(up_val, cv_ref, cs)
        gate = conv(up_gate, cg_ref, cs)
        act = (gate * jax.nn.sigmoid(gate) * val).astype(BF16)
        down = _dot(act, wd_ref[cs, :])
        part = down if part is None else part + down
    o_ref[...] += part

    @pl.when(n == pl.num_programs(2) - 1)
    def _():
        o_ref[...] = _residual_ln(x_ref[...], o_ref[...], g_ref[...], lng_ref[...], lnb_ref[...], alpha)


def _ffn_call(x, sc, sh, w_up, conv_ff, w_down, g, ln_g, ln_b, alpha, layer):
    b, s, d = x.shape
    dff = w_down.shape[1]
    tm = _pick(s, (1024, 512, 256))
    tn = _pick(dff, (512, 256))
    nff = dff // tn
    halo_blocks = tm // ROW_HALO
    return pl.pallas_call(
        functools.partial(_ffn_kernel, alpha=alpha),
        grid=(b, s // tm, nff),
        in_specs=[
            pl.BlockSpec((None, tm, d), lambda i, j, k: (i, j, 0), pipeline_mode=pl.Buffered(1)),
            pl.BlockSpec((None, ROW_HALO, d), lambda i, j, k: (i, jnp.maximum(j * halo_blocks - 1, 0), 0)),
            pl.BlockSpec((None, 1, d), lambda i, j, k: (i, 0, 0)),
            pl.BlockSpec((None, 1, d), lambda i, j, k: (i, 0, 0)),
            pl.BlockSpec((None, d, tn), lambda i, j, k: (layer, 0, k)),
            pl.BlockSpec((None, d, tn), lambda i, j, k: (layer, 0, nff + k)),
            pl.BlockSpec((None, FFN_CONV, tn), lambda i, j, k: (layer, 0, k)),
            pl.BlockSpec((None, FFN_CONV, tn), lambda i, j, k: (layer, 0, nff + k)),
            pl.BlockSpec((None, tn, d), lambda i, j, k: (layer, k, 0)),
            pl.BlockSpec((None, 1, d), lambda i, j, k: (i, 0, 0)),
            pl.BlockSpec((1, d), lambda i, j, k: (0, 0)),
            pl.BlockSpec((1, d), lambda i, j, k: (0, 0)),
        ],
        out_specs=pl.BlockSpec((None, tm, d), lambda i, j, k: (i, j, 0)),
        out_shape=jax.ShapeDtypeStruct((b, s, d), F32),
        scratch_shapes=[pltpu.VMEM((tm + ROW_HALO, d), BF16)],
        compiler_params=_params("parallel", "parallel", "arbitrary", vmem_limit_bytes=FFN_VMEM_LIMIT_BYTES),
        name="ffn_conv_gate_down_ln",
    )(x, x, sc, sh, w_up, w_up, conv_ff, conv_ff, w_down, g, ln_g, ln_b)


def kernel(x, c, w_ada, b_ada, w_in, conv_ml, pool_w, pool_scale, ig_bias, fg_bias, w_branch, w_out,
           w_up, conv_ff, w_down, ln_g, ln_b):
    b, s, d = x.shape
    depth = w_in.shape[0]
    alpha = (2.0 * depth) ** 0.25

    w_main = jnp.concatenate([w_in[:, :, :IF_COL], w_in[:, :, IF_COL + 2 * ML_HEADS:]], axis=2).astype(BF16)
    w_if = w_in[:, :, IF_COL:IF_COL + IF_PAD].astype(BF16)
    pool_w, w_branch, w_out, w_up, w_down = (a.astype(BF16) for a in (pool_w, w_branch, w_out, w_up, w_down))
    pool_scale = pool_scale.reshape(depth, 1, D_BRANCH)

    mod = _mod_call(c, w_ada, b_ada)
    for l in range(depth):
        sh1, sc1, g1, sh2, sc2, g2 = (mod[l][:, None, i * d:(i + 1) * d] for i in range(6))

        main, ifg = _inproj_call(x, sc1, sh1, w_main, w_if, l)
        ifg_t = jnp.swapaxes(ifg[:, :, :2 * ML_HEADS], 1, 2)
        gate_bias = jnp.concatenate([ig_bias[l], fg_bias[l]])
        bias_col = jnp.pad(gate_bias, (0, IF_PAD - 2 * ML_HEADS)).reshape(1, IF_PAD)
        bias_row = gate_bias.reshape(2 * ML_HEADS, 1)

        y_sb = _sb_call(main)
        y_ml = _mlstm_call(main, ifg, ifg_t, conv_ml[l], bias_col, bias_row)
        x = _mixer_out_call(main, pool_w, pool_scale, y_sb, y_ml, w_branch, w_out, x, g1,
                            ln_g[l, 0].reshape(1, d), ln_b[l, 0].reshape(1, d), alpha, l)
        x = _ffn_call(x, sc2, sh2, w_up, conv_ff, w_down, g2,
                      ln_g[l, 1].reshape(1, d), ln_b[l, 1].reshape(1, d), alpha, l)
    return x
```

```python
import functools

import jax
import jax.numpy as jnp
from jax import lax
from jax.experimental import pallas as pl
from jax.experimental.pallas import tpu as pltpu

F32 = jnp.float32
BF16 = jnp.bfloat16

D_BRANCH = 1024
POOL_WINDOWS = (2, 4, 8, 16)
POOL_GROUP_DIM = 256
POOL_HALO = 16
SB_HEADS = 8
SB_HEAD_DIM = 128
ML_HEADS = 4
ML_HEAD_DIM = 256
ML_CONV = 4
FFN_CONV = 3
FFN_SUB = 512
LN_EPS = 1e-5
GATE_COL = 8 * D_BRANCH
IF_COL = 8 * D_BRANCH
IF_PAD = 128
ROW_HALO = 16
CONV_HALO = 8

LOG2_E = 1.4426950408889634

SB_ZERO_BITS = 127.0

VMEM_LIMIT_BYTES = 48 * 1024 * 1024
INPROJ_VMEM_LIMIT_BYTES = 56 * 1024 * 1024
FFN_VMEM_LIMIT_BYTES = 58 * 1024 * 1024


def _params(*sem, vmem_limit_bytes=VMEM_LIMIT_BYTES):
    return pltpu.CompilerParams(dimension_semantics=sem, vmem_limit_bytes=vmem_limit_bytes)


def _pick(n, cands):
    for c in cands:
        if n % c == 0:
            return c
    raise ValueError(f"no tile in {cands} divides {n}")


def _dot(a, b):
    return jnp.dot(a, b, preferred_element_type=F32)


def _dot_nt(a, b):
    return lax.dot_general(a, b, (((1,), (1,)), ((), ())), preferred_element_type=F32)


def _dot_tn(a, b):
    return lax.dot_general(a, b, (((0,), (0,)), ((), ())), preferred_element_type=F32)


def _log_sigmoid_parts(z):
    return jnp.log1p(jnp.exp(-jnp.abs(z)))


def _split_bf16(a, terms):
    parts = []
    rem = a
    for _ in range(terms):
        p = rem.astype(BF16)
        parts.append(p)
        rem = rem - p.astype(F32)
    return parts


def _mod_kernel(c_ref, w_ref, b_ref, o_ref):
    c = c_ref[...]
    ca = (c * jax.nn.sigmoid(c)).astype(BF16)
    o_ref[...] = _dot(ca, w_ref[...].astype(BF16)) + b_ref[...]


def _mod_call(c, w_ada, b_ada):
    depth, d, n = w_ada.shape
    b = c.shape[0]
    tn = _pick(n, (1024, 512, 256, 128))
    return pl.pallas_call(
        _mod_kernel,
        grid=(depth, n // tn),
        in_specs=[
            pl.BlockSpec((b, d), lambda l, j: (0, 0)),
            pl.BlockSpec((None, d, tn), lambda l, j: (l, 0, j)),
            pl.BlockSpec((None, 1, tn), lambda l, j: (l, 0, j)),
        ],
        out_specs=pl.BlockSpec((None, b, tn), lambda l, j: (l, 0, j)),
        out_shape=jax.ShapeDtypeStruct((depth, b, n), F32),
        compiler_params=_params("parallel", "parallel"),
        name="adaln_mod",
    )(c, w_ada, b_ada.reshape(depth, 1, n))


def _inproj_kernel(x_ref, sc_ref, sh_ref, w_ref, wif_ref, o_ref, oif_ref, h_ref):
    @pl.when(pl.program_id(2) == 0)
    def _():
        h = (x_ref[...] * (1.0 + sc_ref[...]) + sh_ref[...]).astype(h_ref.dtype)
        h_ref[...] = h
        oif_ref[...] = _dot(h, wif_ref[...])

    o_ref[...] = _dot(h_ref[...], w_ref[...]).astype(o_ref.dtype)


def _inproj_call(x, sc, sh, w, w_if, layer):
    b, s, d = x.shape
    n = w.shape[2]
    tm = _pick(s, (1024, 512, 256))
    tn = _pick(n, (2048, 1024, 512, 256, 128))
    return pl.pallas_call(
        _inproj_kernel,
        grid=(b, s // tm, n // tn),
        in_specs=[
            pl.BlockSpec((None, tm, d), lambda i, j, k: (i, j, 0)),
            pl.BlockSpec((None, 1, d), lambda i, j, k: (i, 0, 0)),
            pl.BlockSpec((None, 1, d), lambda i, j, k: (i, 0, 0)),
            pl.BlockSpec((None, d, tn), lambda i, j, k: (layer, 0, k)),
            pl.BlockSpec((None, d, IF_PAD), lambda i, j, k: (layer, 0, 0)),
        ],
        out_specs=[pl.BlockSpec((None, tm, tn), lambda i, j, k: (i, j, k)),
                   pl.BlockSpec((None, tm, IF_PAD), lambda i, j, k: (i, j, 0))],
        out_shape=[jax.ShapeDtypeStruct((b, s, n), BF16), jax.ShapeDtypeStruct((b, s, IF_PAD), F32)],
        scratch_shapes=[pltpu.VMEM((tm, d), BF16)],
        compiler_params=_params("parallel", "parallel", "arbitrary", vmem_limit_bytes=INPROJ_VMEM_LIMIT_BYTES),
        name="in_proj",
    )(x, sc, sh, w, w_if)


def _pool_tile(a_ref, w_ref, scale_ref, prev_ref, st):
    t = a_ref.shape[0]

    @pl.when(st == 0)
    def _():
        prev_ref[...] = jnp.zeros_like(prev_ref)

    row = lax.broadcasted_iota(jnp.int32, (t, t), 0)
    col = lax.broadcasted_iota(jnp.int32, (t, t), 1)
    prow = lax.broadcasted_iota(jnp.int32, (t, POOL_HALO), 0)
    pcol = lax.broadcasted_iota(jnp.int32, (t, POOL_HALO), 1) - POOL_HALO
    t_abs = st * t + lax.broadcasted_iota(jnp.int32, (t, 1), 0)
    outs = []
    for g, win in enumerate(POOL_WINDOWS):
        cs = slice(g * POOL_GROUP_DIM, (g + 1) * POOL_GROUP_DIM)
        a = a_ref[:, cs]
        band = jnp.where((col <= row) & (col > row - win), 1.0, 0.0).astype(BF16)
        pband = jnp.where(pcol > prow - win, 1.0, 0.0).astype(BF16)
        wsum = _dot(band, a) + _dot(pband, prev_ref[:, cs])
        cnt = jnp.minimum(t_abs + 1, win).astype(F32)
        diff = (wsum / cnt - a.astype(F32)).astype(BF16)
        outs.append((_dot(diff, w_ref[g]) * scale_ref[:, cs]).astype(BF16))
    prev_ref[...] = a_ref[t - POOL_HALO:, :]
    return jnp.concatenate(outs, axis=1)


def _sb_kernel(q_ref, k_ref, v_ref, o_ref, *, tq):
    qb = pl.program_id(1)
    scale = SB_HEAD_DIM ** -0.5 * LOG2_E
    row = lax.broadcasted_iota(jnp.int32, (tq, tq), 0)
    col = lax.broadcasted_iota(jnp.int32, (tq, tq), 1)
    below = row > col
    after_sum = jnp.where(below, 1.0, 0.0).astype(BF16)

    def sweep(rows, accs, rs):
        diag = accs is None
        heads = [slice(h * SB_HEAD_DIM, (h + 1) * SB_HEAD_DIM) for h in range(SB_HEADS)]
        zs = [_dot_nt(q_ref[:, hs], k_ref[rows, hs]) * scale for hs in heads]
        log_betas, drops, his, los = [], [], [], []
        for z in zs:
            l = jnp.log2(1.0 + jnp.exp2(-jnp.abs(z)))
            drop = jnp.maximum(z, 0.0) + l
            if diag:
                drop = jnp.where(below, drop, 0.0)
            hi, lo = _split_bf16(drop, 2)
            log_betas.append(jnp.minimum(z, 0.0) - l)
            drops.append(drop)
            his.append(hi)
            los.append(lo)
        sums = _dot(jnp.concatenate(his + los, axis=0), after_sum)
        attns, new_rs, rmin = [], [], None
        for h in range(SB_HEADS):
            after = sums[h * tq:(h + 1) * tq] + sums[(SB_HEADS + h) * tq:(SB_HEADS + h + 1) * tq]
            rowsum = jnp.sum(drops[h], axis=1, keepdims=True)
            if diag:
                attn = jnp.where(below, jnp.exp2(log_betas[h] - after), 0.0)
                rn = rowsum
            else:
                attn = jnp.exp2(log_betas[h] - (after + rs[h]))
                rn = rs[h] + rowsum
            attns.append(attn.astype(BF16))
            new_rs.append(rn)
            rmin = rn if rmin is None else jnp.minimum(rmin, rn)
        pvs = [_dot(attns[h], v_ref[rows, hs]) for h, hs in enumerate(heads)]
        new_accs = pvs if diag else [a + p for a, p in zip(accs, pvs)]
        return tuple(new_accs), tuple(new_rs), jnp.min(rmin)

    accs0, rs0, rmin0 = sweep(pl.ds(pl.multiple_of(qb * tq, tq), tq), None, None)

    def cond(carry):
        kb, rmin, _, _ = carry
        return jnp.logical_and(kb >= 0, rmin < SB_ZERO_BITS)

    def body(carry):
        kb, _, accs, rs = carry
        accs, rs, rmin = sweep(pl.ds(pl.multiple_of(kb * tq, tq), tq), accs, rs)
        return kb - 1, rmin, accs, rs

    _, _, accs, _ = lax.while_loop(cond, body, (qb - 1, rmin0, accs0, rs0))
    for h in range(SB_HEADS):
        o_ref[:, h * SB_HEAD_DIM:(h + 1) * SB_HEAD_DIM] = accs[h].astype(o_ref.dtype)


def _sb_call(main):
    b, s, _ = main.shape
    tq = 128
    return pl.pallas_call(
        functools.partial(_sb_kernel, tq=tq),
        grid=(b, s // tq),
        in_specs=[
            pl.BlockSpec((None, tq, D_BRANCH), lambda i, j: (i, j, 1)),
            pl.BlockSpec((None, s, D_BRANCH), lambda i, j: (i, 0, 2)),
            pl.BlockSpec((None, s, D_BRANCH), lambda i, j: (i, 0, 3)),
        ],
        out_specs=pl.BlockSpec((None, tq, D_BRANCH), lambda i, j: (i, j, 0)),
        out_shape=jax.ShapeDtypeStruct((b, s, D_BRANCH), BF16),
        compiler_params=_params("parallel", "arbitrary"),
        name="stick_breaking",
    )(main, main, main)


def _mlstm_kernel(qk_ref, v_ref, og_ref, if_ref, ift_ref, cw_ref, bcol_ref, brow_ref, o_ref,
                  ext_ref, ct_ref, n_ref, m_ref, *, chunk):
    L = chunk

    @pl.when(pl.program_id(1) == 0)
    def _():
        ext_ref[0:CONV_HALO, :] = jnp.zeros((CONV_HALO, ext_ref.shape[1]), F32)
        ct_ref[...] = jnp.zeros_like(ct_ref)
        n_ref[...] = jnp.zeros_like(n_ref)
        m_ref[...] = jnp.zeros_like(m_ref)

    ext_ref[CONV_HALO:, :] = qk_ref[...].astype(F32)
    ext = ext_ref[...]
    cw = cw_ref[...]
    conv = ext * cw[ML_CONV - 1:ML_CONV, :]
    for j in range(1, ML_CONV):
        conv = conv + pltpu.roll(ext, j, axis=0) * cw[ML_CONV - 1 - j:ML_CONV - j, :]
    conv = conv[CONV_HALO:, :]
    qk = conv * jax.nn.sigmoid(conv)
    ext_ref[0:CONV_HALO, :] = ext_ref[L:L + CONV_HALO, :]

    if_c = if_ref[...] + bcol_ref[...]
    if_r = ift_ref[...] + brow_ref[...]
    logf_c = jnp.minimum(if_c, 0.0) - _log_sigmoid_parts(if_c)
    logf_r = jnp.minimum(if_r, 0.0) - _log_sigmoid_parts(if_r)
    row = lax.broadcasted_iota(jnp.int32, (L, L), 0)
    col = lax.broadcasted_iota(jnp.int32, (L, L), 1)
    causal = col <= row
    incl_lower = jnp.where(causal, 1.0, 0.0).astype(BF16)
    incl_upper = jnp.where(row <= col, 1.0, 0.0).astype(BF16)
    b_c = sum(_dot(incl_lower, p) for p in _split_bf16(logf_c, 3))
    b_r = sum(_dot(p, incl_upper) for p in _split_bf16(logf_r, 3))

    for h in range(ML_HEADS):
        hs = slice(h * ML_HEAD_DIM, (h + 1) * ML_HEAD_DIM)
        ks = slice(D_BRANCH + h * ML_HEAD_DIM, D_BRANCH + (h + 1) * ML_HEAD_DIM)
        bcol = b_c[:, ML_HEADS + h:ML_HEADS + h + 1]
        icol = if_c[:, h:h + 1]
        brow = b_r[ML_HEADS + h:ML_HEADS + h + 1, :]
        irow = if_r[h:h + 1, :]
        m_prev = m_ref[h:h + 1, 0:1]

        d = jnp.where(causal, bcol - brow + irow, -jnp.inf)
        inter = bcol + m_prev
        m_t = jnp.maximum(inter, jnp.max(d, axis=1, keepdims=True))
        w_intra = jnp.exp(d - m_t)
        w_inter = jnp.exp(inter - m_t)

        q = qk[:, hs].astype(BF16)
        k32 = qk[:, ks] * (ML_HEAD_DIM ** -0.5)
        v = v_ref[:, hs]
        ct = ct_ref[h]
        nrow = n_ref[h:h + 1, :]
        s = _dot_nt(q, k32.astype(BF16)) * w_intra
        num = w_inter * _dot(q, ct.astype(BF16)) + _dot(s.astype(BF16), v)
        den = (w_inter * jnp.sum(q.astype(F32) * nrow, axis=1, keepdims=True)
               + jnp.sum(s, axis=1, keepdims=True))
        hid = num / jnp.maximum(jnp.abs(den), jnp.exp(-m_t))
        og = og_ref[:, hs].astype(F32)
        o_ref[:, hs] = (hid * jax.nn.sigmoid(og)).astype(o_ref.dtype)

        b_last = bcol[L - 1:L, :]
        dec = b_last - bcol + icol
        m_new = jnp.maximum(b_last + m_prev, jnp.max(dec, axis=0, keepdims=True))
        w_s = jnp.exp(dec - m_new)
        w_prev = jnp.exp(b_last + m_prev - m_new)
        kw = k32 * w_s
        ct_ref[h] = w_prev * ct + _dot_tn(kw.astype(BF16), v)
        n_ref[h:h + 1, :] = w_prev * nrow + jnp.sum(kw, axis=0, keepdims=True)
        m_ref[h:h + 1, :] = jnp.broadcast_to(m_new, (1, m_ref.shape[1]))


def _mlstm_call(main, ifg, ifg_t, conv_w, bias_col, bias_row):
    b, s, _ = main.shape
    chunk = 256
    return pl.pallas_call(
        functools.partial(_mlstm_kernel, chunk=chunk),
        grid=(b, s // chunk),
        in_specs=[
            pl.BlockSpec((None, chunk, 2 * D_BRANCH), lambda i, j: (i, j, 2)),
            pl.BlockSpec((None, chunk, D_BRANCH), lambda i, j: (i, j, 6)),
            pl.BlockSpec((None, chunk, D_BRANCH), lambda i, j: (i, j, 7)),
            pl.BlockSpec((None, chunk, IF_PAD), lambda i, j: (i, j, 0)),
            pl.BlockSpec((None, 2 * ML_HEADS, chunk), lambda i, j: (i, 0, j)),
            pl.BlockSpec((ML_CONV, 2 * D_BRANCH), lambda i, j: (0, 0)),
            pl.BlockSpec((1, IF_PAD), lambda i, j: (0, 0)),
            pl.BlockSpec((2 * ML_HEADS, 1), lambda i, j: (0, 0)),
        ],
        out_specs=pl.BlockSpec((None, chunk, D_BRANCH), lambda i, j: (i, j, 0)),
        out_shape=jax.ShapeDtypeStruct((b, s, D_BRANCH), BF16),
        scratch_shapes=[
            pltpu.VMEM((chunk + CONV_HALO, 2 * D_BRANCH), F32),
            pltpu.VMEM((ML_HEADS, ML_HEAD_DIM, ML_HEAD_DIM), F32),
            pltpu.VMEM((2 * ML_HEADS, ML_HEAD_DIM), F32),
            pltpu.VMEM((2 * ML_HEADS, 128), F32),
        ],
        compiler_params=_params("parallel", "arbitrary"),
        name="mlstm",
    )(main, main, main, ifg, ifg_t, conv_w, bias_col, bias_row)


def _residual_ln(x, y, g, ln_g, ln_b, alpha):
    r = alpha * x + (1.0 + g) * y
    mu = jnp.mean(r, axis=-1, keepdims=True)
    cen = r - mu
    var = jnp.mean(cen * cen, axis=-1, keepdims=True)
    return cen * lax.rsqrt(var + LN_EPS) * ln_g + ln_b


def _mixer_out_kernel(ap_ref, pw_ref, ps_ref, ys_ref, ym_ref, g0_ref, g1_ref, g2_ref, wb_ref, wo_ref, x_ref,
                      g_ref, lng_ref, lnb_ref, o_ref, prev_ref, *, alpha):
    y_pool = _pool_tile(ap_ref, pw_ref, ps_ref, prev_ref, pl.program_id(1))
    acc = None
    for i, (y, gl_ref) in enumerate(((y_pool, g0_ref), (ys_ref[...], g1_ref), (ym_ref[...], g2_ref))):
        term = jax.nn.sigmoid(gl_ref[...].astype(F32)) * _dot(y, wb_ref[i])
        acc = term if acc is None else acc + term
    y = _dot(acc.astype(BF16), wo_ref[...])
    o_ref[...] = _residual_ln(x_ref[...], y, g_ref[...], lng_ref[...], lnb_ref[...], alpha)


def _resident_layer(stacked_shape, layer):
    tail = (0,) * (len(stacked_shape) - 1)
    return pl.BlockSpec((None,) + tuple(stacked_shape[1:]), lambda *_: (layer,) + tail,
                        pipeline_mode=pl.Buffered(1))


def _mixer_out_call(main, pool_w, pool_scale, y_sb, y_ml, w_branch, w_out, x, g, ln_g, ln_b, alpha, layer):
    b, s, d = x.shape
    tm = 256
    y_spec = pl.BlockSpec((None, tm, D_BRANCH), lambda i, j: (i, j, 0))

    def gate_spec(br):
        blk = GATE_COL // d + br
        return pl.BlockSpec((None, tm, d), lambda i, j: (i, j, blk))

    return pl.pallas_call(
        functools.partial(_mixer_out_kernel, alpha=alpha),
        grid=(b, s // tm),
        in_specs=[y_spec, _resident_layer(pool_w.shape, layer), _resident_layer(pool_scale.shape, layer),
                  y_spec, y_spec, gate_spec(0), gate_spec(1), gate_spec(2),
                  _resident_layer(w_branch.shape, layer), _resident_layer(w_out.shape, layer),
                  pl.BlockSpec((None, tm, d), lambda i, j: (i, j, 0)),
                  pl.BlockSpec((None, 1, d), lambda i, j: (i, 0, 0)),
                  pl.BlockSpec((1, d), lambda i, j: (0, 0)),
                  pl.BlockSpec((1, d), lambda i, j: (0, 0))],
        out_specs=pl.BlockSpec((None, tm, d), lambda i, j: (i, j, 0)),
        out_shape=jax.ShapeDtypeStruct((b, s, d), F32),
        scratch_shapes=[pltpu.VMEM((POOL_HALO, D_BRANCH), BF16)],
        compiler_params=_params("parallel", "arbitrary"),
        name="mixer_out_ln",
    )(main, pool_w, pool_scale, y_sb, y_ml, main, main, main, w_branch, w_out, x, g, ln_g, ln_b)


def _ffn_kernel(x_ref, xh_ref, sc_ref, sh_ref, wv_ref, wg_ref, cv_ref, cg_ref, wd_ref, g_ref,
                lng_ref, lnb_ref, o_ref, h_ref, *, alpha):
    n = pl.program_id(2)

    @pl.when(n == 0)
    def _():
        sc = 1.0 + sc_ref[...]
        sh = sh_ref[...]
        halo = jnp.where(pl.program_id(1) > 0, xh_ref[...] * sc + sh, 0.0)
        h_ref[0:ROW_HALO, :] = halo.astype(h_ref.dtype)
        h_ref[ROW_HALO:, :] = (x_ref[...] * sc + sh).astype(h_ref.dtype)
        o_ref[...] = jnp.zeros_like(o_ref)

    h = h_ref[...]

    def conv(up, c_ref, cs):
        c = c_ref[:, cs]
        out = up * c[FFN_CONV - 1:FFN_CONV, :]
        for j in range(1, FFN_CONV):
            out = out + pltpu.roll(up, j, axis=0) * c[FFN_CONV - 1 - j:FFN_CONV - j, :]
        return out[ROW_HALO:, :]

    subs = [slice(j * FFN_SUB, (j + 1) * FFN_SUB) for j in range(wv_ref.shape[1] // FFN_SUB)]
    ups = [(_dot(h, wv_ref[:, cs]), _dot(h, wg_ref[:, cs])) for cs in subs]
    part = None
    for cs, (up_val, up_gate) in zip(subs, ups):
        val = conv(up_val, cv_ref, cs)
        gate = conv(up_gate, cg_ref, cs)
        act = (gate * jax.nn.sigmoid(gate) * val).astype(BF16)
        down = _dot(act, wd_ref[cs, :])
        part = down if part is None else part + down
    o_ref[...] += part

    @pl.when(n == pl.num_programs(2) - 1)
    def _():
        o_ref[...] = _residual_ln(x_ref[...], o_ref[...], g_ref[...], lng_ref[...], lnb_ref[...], alpha)


def _ffn_call(x, sc, sh, w_up, conv_ff, w_down, g, ln_g, ln_b, alpha, layer):
    b, s, d = x.shape
    dff = w_down.shape[1]
    tm = _pick(s, (1024, 512, 256))
    tn = _pick(dff, (512, 256))
    nff = dff // tn
    halo_blocks = tm // ROW_HALO
    return pl.pallas_call(
        functools.partial(_ffn_kernel, alpha=alpha),
        grid=(b, s // tm, nff),
        in_specs=[
            pl.BlockSpec((None, tm, d), lambda i, j, k: (i, j, 0), pipeline_mode=pl.Buffered(1)),
            pl.BlockSpec((None, ROW_HALO, d), lambda i, j, k: (i, jnp.maximum(j * halo_blocks - 1, 0), 0)),
            pl.BlockSpec((None, 1, d), lambda i, j, k: (i, 0, 0)),
            pl.BlockSpec((None, 1, d), lambda i, j, k: (i, 0, 0)),
            pl.BlockSpec((None, d, tn), lambda i, j, k: (layer, 0, k)),
            pl.BlockSpec((None, d, tn), lambda i, j, k: (layer, 0, nff + k)),
            pl.BlockSpec((None, FFN_CONV, tn), lambda i, j, k: (layer, 0, k)),
            pl.BlockSpec((None, FFN_CONV, tn), lambda i, j, k: (layer, 0, nff + k)),
            pl.BlockSpec((None, tn, d), lambda i, j, k: (layer, k, 0)),
            pl.BlockSpec((None, 1, d), lambda i, j, k: (i, 0, 0)),
            pl.BlockSpec((1, d), lambda i, j, k: (0, 0)),
            pl.BlockSpec((1, d), lambda i, j, k: (0, 0)),
        ],
        out_specs=pl.BlockSpec((None, tm, d), lambda i, j, k: (i, j, 0)),
        out_shape=jax.ShapeDtypeStruct((b, s, d), F32),
        scratch_shapes=[pltpu.VMEM((tm + ROW_HALO, d), BF16)],
        compiler_params=_params("parallel", "parallel", "arbitrary", vmem_limit_bytes=FFN_VMEM_LIMIT_BYTES),
        name="ffn_conv_gate_down_ln",
    )(x, x, sc, sh, w_up, w_up, conv_ff, conv_ff, w_down, g, ln_g, ln_b)


def kernel(x, c, w_ada, b_ada, w_in, conv_ml, pool_w, pool_scale, ig_bias, fg_bias, w_branch, w_out,
           w_up, conv_ff, w_down, ln_g, ln_b):
    b, s, d = x.shape
    depth = w_in.shape[0]
    alpha = (2.0 * depth) ** 0.25

    w_main = jnp.concatenate([w_in[:, :, :IF_COL], w_in[:, :, IF_COL + 2 * ML_HEADS:]], axis=2).astype(BF16)
    w_if = w_in[:, :, IF_COL:IF_COL + IF_PAD].astype(BF16)
    pool_w, w_branch, w_out, w_up, w_down = (a.astype(BF16) for a in (pool_w, w_branch, w_out, w_up, w_down))
    pool_scale = pool_scale.reshape(depth, 1, D_BRANCH)

    mod = _mod_call(c, w_ada, b_ada)
    for l in range(depth):
        sh1, sc1, g1, sh2, sc2, g2 = (mod[l][:, None, i * d:(i + 1) * d] for i in range(6))

        main, ifg = _inproj_call(x, sc1, sh1, w_main, w_if, l)
        ifg_t = jnp.swapaxes(ifg[:, :, :2 * ML_HEADS], 1, 2)
        gate_bias = jnp.concatenate([ig_bias[l], fg_bias[l]])
        bias_col = jnp.pad(gate_bias, (0, IF_PAD - 2 * ML_HEADS)).reshape(1, IF_PAD)
        bias_row = gate_bias.reshape(2 * ML_HEADS, 1)

        y_sb = _sb_call(main)
        y_ml = _mlstm_call(main, ifg, ifg_t, conv_ml[l], bias_col, bias_row)
        x = _mixer_out_call(main, pool_w, pool_scale, y_sb, y_ml, w_branch, w_out, x, g1,
                            ln_g[l, 0].reshape(1, d), ln_b[l, 0].reshape(1, d), alpha, l)
        x = _ffn_call(x, sc2, sh2, w_up, conv_ff, w_down, g2,
                      ln_g[l, 1].reshape(1, d), ln_b[l, 1].reshape(1, d), alpha, l)
    return x
```

```python
import functools

import jax
import jax.numpy as jnp
from jax import lax
from jax.experimental import pallas as pl
from jax.experimental.pallas import tpu as pltpu

F32 = jnp.float32
BF16 = jnp.bfloat16

D_BRANCH = 1024
POOL_WINDOWS = (2, 4, 8, 16)
POOL_GROUP_DIM = 256
POOL_HALO = 16
SB_HEADS = 8
SB_HEAD_DIM = 128
SB_STEP_BLOCKS = 4
ML_HEADS = 4
ML_HEAD_DIM = 256
ML_CONV = 4
ML_STEP_CHUNKS = 2
FFN_CONV = 3
FFN_SUB = 512
LN_EPS = 1e-5
GATE_COL = 8 * D_BRANCH
IF_COL = 8 * D_BRANCH
IF_PAD = 128
ROW_HALO = 16
CONV_HALO = 8

LOG2_E = 1.4426950408889634

SB_ZERO_BITS = 127.0

VMEM_LIMIT_BYTES = 48 * 1024 * 1024
INPROJ_VMEM_LIMIT_BYTES = 56 * 1024 * 1024
FFN_VMEM_LIMIT_BYTES = 58 * 1024 * 1024


def _params(*sem, vmem_limit_bytes=VMEM_LIMIT_BYTES):
    return pltpu.CompilerParams(dimension_semantics=sem, vmem_limit_bytes=vmem_limit_bytes)


def _pick(n, cands):
    for c in cands:
        if n % c == 0:
            return c
    raise ValueError(f"no tile in {cands} divides {n}")


def _dot(a, b):
    return jnp.dot(a, b, preferred_element_type=F32)


def _dot_nt(a, b):
    return lax.dot_general(a, b, (((1,), (1,)), ((), ())), preferred_element_type=F32)


def _dot_tn(a, b):
    return lax.dot_general(a, b, (((0,), (0,)), ((), ())), preferred_element_type=F32)


def _log_sigmoid_parts(z):
    return jnp.log1p(jnp.exp(-jnp.abs(z)))


def _split_bf16(a, terms):
    parts = []
    rem = a
    for _ in range(terms):
        p = rem.astype(BF16)
        parts.append(p)
        rem = rem - p.astype(F32)
    return parts


def _mod_kernel(c_ref, w_ref, b_ref, o_ref):
    c = c_ref[...]
    ca = (c * jax.nn.sigmoid(c)).astype(BF16)
    o_ref[...] = _dot(ca, w_ref[...].astype(BF16)) + b_ref[...]


def _mod_call(c, w_ada, b_ada):
    depth, d, n = w_ada.shape
    b = c.shape[0]
    tn = _pick(n, (1024, 512, 256, 128))
    return pl.pallas_call(
        _mod_kernel,
        grid=(depth, n // tn),
        in_specs=[
            pl.BlockSpec((b, d), lambda l, j: (0, 0)),
            pl.BlockSpec((None, d, tn), lambda l, j: (l, 0, j)),
            pl.BlockSpec((None, 1, tn), lambda l, j: (l, 0, j)),
        ],
        out_specs=pl.BlockSpec((None, b, tn), lambda l, j: (l, 0, j)),
        out_shape=jax.ShapeDtypeStruct((depth, b, n), F32),
        compiler_params=_params("parallel", "parallel"),
        name="adaln_mod",
    )(c, w_ada, b_ada.reshape(depth, 1, n))


def _inproj_kernel(x_ref, sc_ref, sh_ref, w_ref, wif_ref, o_ref, oif_ref, h_ref):
    @pl.when(pl.program_id(2) == 0)
    def _():
        h = (x_ref[...] * (1.0 + sc_ref[...]) + sh_ref[...]).astype(h_ref.dtype)
        h_ref[...] = h
        oif_ref[...] = _dot(h, wif_ref[...])

    o_ref[...] = _dot(h_ref[...], w_ref[...]).astype(o_ref.dtype)


def _inproj_call(x, sc, sh, w, w_if, layer):
    b, s, d = x.shape
    n = w.shape[2]
    tm = _pick(s, (1024, 512, 256))
    tn = _pick(n, (2048, 1024, 512, 256, 128))
    return pl.pallas_call(
        _inproj_kernel,
        grid=(b, s // tm, n // tn),
        in_specs=[
            pl.BlockSpec((None, tm, d), lambda i, j, k: (i, j, 0)),
            pl.BlockSpec((None, 1, d), lambda i, j, k: (i, 0, 0)),
            pl.BlockSpec((None, 1, d), lambda i, j, k: (i, 0, 0)),
            pl.BlockSpec((None, d, tn), lambda i, j, k: (layer, 0, k)),
            pl.BlockSpec((None, d, IF_PAD), lambda i, j, k: (layer, 0, 0)),
        ],
        out_specs=[pl.BlockSpec((None, tm, tn), lambda i, j, k: (i, j, k)),
                   pl.BlockSpec((None, tm, IF_PAD), lambda i, j, k: (i, j, 0))],
        out_shape=[jax.ShapeDtypeStruct((b, s, n), BF16), jax.ShapeDtypeStruct((b, s, IF_PAD), F32)],
        scratch_shapes=[pltpu.VMEM((tm, d), BF16)],
        compiler_params=_params("parallel", "parallel", "arbitrary", vmem_limit_bytes=INPROJ_VMEM_LIMIT_BYTES),
        name="in_proj",
    )(x, sc, sh, w, w_if)


def _pool_tile(a_ref, w_ref, scale_ref, prev_ref, st):
    t = a_ref.shape[0]

    @pl.when(st == 0)
    def _():
        prev_ref[...] = jnp.zeros_like(prev_ref)

    row = lax.broadcasted_iota(jnp.int32, (t, t), 0)
    col = lax.broadcasted_iota(jnp.int32, (t, t), 1)
    prow = lax.broadcasted_iota(jnp.int32, (t, POOL_HALO), 0)
    pcol = lax.broadcasted_iota(jnp.int32, (t, POOL_HALO), 1) - POOL_HALO
    t_abs = st * t + lax.broadcasted_iota(jnp.int32, (t, 1), 0)
    outs = []
    for g, win in enumerate(POOL_WINDOWS):
        cs = slice(g * POOL_GROUP_DIM, (g + 1) * POOL_GROUP_DIM)
        a = a_ref[:, cs]
        band = jnp.where((col <= row) & (col > row - win), 1.0, 0.0).astype(BF16)
        pband = jnp.where(pcol > prow - win, 1.0, 0.0).astype(BF16)
        wsum = _dot(band, a) + _dot(pband, prev_ref[:, cs])
        cnt = jnp.minimum(t_abs + 1, win).astype(F32)
        diff = (wsum / cnt - a.astype(F32)).astype(BF16)
        outs.append((_dot(diff, w_ref[g]) * scale_ref[:, cs]).astype(BF16))
    prev_ref[...] = a_ref[t - POOL_HALO:, :]
    return jnp.concatenate(outs, axis=1)


def _sb_kernel(q_ref, k_ref, v_ref, o_ref, *, tq):
    scale = SB_HEAD_DIM ** -0.5 * LOG2_E
    row = lax.broadcasted_iota(jnp.int32, (tq, tq), 0)
    col = lax.broadcasted_iota(jnp.int32, (tq, tq), 1)
    below = row > col
    after_sum = jnp.where(below, 1.0, 0.0).astype(BF16)

    def sweep(qrows, rows, accs, rs):
        diag = accs is None
        heads = [slice(h * SB_HEAD_DIM, (h + 1) * SB_HEAD_DIM) for h in range(SB_HEADS)]
        zs = [_dot_nt(q_ref[qrows, hs], k_ref[rows, hs]) * scale for hs in heads]
        log_betas, drops, his, los = [], [], [], []
        for z in zs:
            l = jnp.log2(1.0 + jnp.exp2(-jnp.abs(z)))
            drop = jnp.maximum(z, 0.0) + l
            if diag:
                drop = jnp.where(below, drop, 0.0)
            hi, lo = _split_bf16(drop, 2)
            log_betas.append(jnp.minimum(z, 0.0) - l)
            drops.append(drop)
            his.append(hi)
            los.append(lo)
        sums = _dot(jnp.concatenate(his + los, axis=0), after_sum)
        attns, new_rs, rmin = [], [], None
        for h in range(SB_HEADS):
            after = sums[h * tq:(h + 1) * tq] + sums[(SB_HEADS + h) * tq:(SB_HEADS + h + 1) * tq]
            rowsum = jnp.sum(drops[h], axis=1, keepdims=True)
            if diag:
                attn = jnp.where(below, jnp.exp2(log_betas[h] - after), 0.0)
                rn = rowsum
            else:
                attn = jnp.exp2(log_betas[h] - (after + rs[h]))
                rn = rs[h] + rowsum
            attns.append(attn.astype(BF16))
            new_rs.append(rn)
            rmin = rn if rmin is None else jnp.minimum(rmin, rn)
        pvs = [_dot(attns[h], v_ref[rows, hs]) for h, hs in enumerate(heads)]
        new_accs = pvs if diag else [a + p for a, p in zip(accs, pvs)]
        return tuple(new_accs), tuple(new_rs), jnp.min(rmin)

    def cond(carry):
        kb, rmin, _, _ = carry
        return jnp.logical_and(kb >= 0, rmin < SB_ZERO_BITS)

    def query_block(i, carry):
        qb = pl.program_id(1) * SB_STEP_BLOCKS + i
        qrows = pl.ds(pl.multiple_of(i * tq, tq), tq)
        accs0, rs0, rmin0 = sweep(qrows, pl.ds(pl.multiple_of(qb * tq, tq), tq), None, None)

        def body(c):
            kb, _, accs, rs = c
            accs, rs, rmin = sweep(qrows, pl.ds(pl.multiple_of(kb * tq, tq), tq), accs, rs)
            return kb - 1, rmin, accs, rs

        _, _, accs, _ = lax.while_loop(cond, body, (qb - 1, rmin0, accs0, rs0))
        for h in range(SB_HEADS):
            o_ref[qrows, h * SB_HEAD_DIM:(h + 1) * SB_HEAD_DIM] = accs[h].astype(o_ref.dtype)
        return carry

    lax.fori_loop(0, SB_STEP_BLOCKS, query_block, 0)


def _sb_call(main):
    b, s, _ = main.shape
    tq = 128
    step = tq * SB_STEP_BLOCKS
    return pl.pallas_call(
        functools.partial(_sb_kernel, tq=tq),
        grid=(b, s // step),
        in_specs=[
            pl.BlockSpec((None, step, D_BRANCH), lambda i, j: (i, j, 1)),
            pl.BlockSpec((None, s, D_BRANCH), lambda i, j: (i, 0, 2)),
            pl.BlockSpec((None, s, D_BRANCH), lambda i, j: (i, 0, 3)),
        ],
        out_specs=pl.BlockSpec((None, step, D_BRANCH), lambda i, j: (i, j, 0)),
        out_shape=jax.ShapeDtypeStruct((b, s, D_BRANCH), BF16),
        compiler_params=_params("parallel", "arbitrary"),
        name="stick_breaking",
    )(main, main, main)


def _mlstm_kernel(qk_ref, v_ref, og_ref, if_ref, ift_ref, cw_ref, bcol_ref, brow_ref, o_ref,
                  ext_ref, ct_ref, n_ref, m_ref, *, chunk):
    @pl.when(pl.program_id(1) == 0)
    def _():
        ext_ref[0:CONV_HALO, :] = jnp.zeros((CONV_HALO, ext_ref.shape[1]), F32)
        ct_ref[...] = jnp.zeros_like(ct_ref)
        n_ref[...] = jnp.zeros_like(n_ref)
        m_ref[...] = jnp.zeros_like(m_ref)

    for c in range(ML_STEP_CHUNKS):
        rows = slice(c * chunk, (c + 1) * chunk)
        _mlstm_chunk(qk_ref.at[rows, :], v_ref.at[rows, :], og_ref.at[rows, :], if_ref.at[rows, :],
                     ift_ref.at[:, rows], cw_ref, bcol_ref, brow_ref, o_ref.at[rows, :],
                     ext_ref, ct_ref, n_ref, m_ref, chunk)


def _mlstm_chunk(qk_ref, v_ref, og_ref, if_ref, ift_ref, cw_ref, bcol_ref, brow_ref, o_ref,
                 ext_ref, ct_ref, n_ref, m_ref, chunk):
    L = chunk

    ext_ref[CONV_HALO:, :] = qk_ref[...].astype(F32)
    ext = ext_ref[...]
    cw = cw_ref[...]
    conv = ext * cw[ML_CONV - 1:ML_CONV, :]
    for j in range(1, ML_CONV):
        conv = conv + pltpu.roll(ext, j, axis=0) * cw[ML_CONV - 1 - j:ML_CONV - j, :]
    conv = conv[CONV_HALO:, :]
    qk = conv * jax.nn.sigmoid(conv)
    ext_ref[0:CONV_HALO, :] = ext_ref[L:L + CONV_HALO, :]

    if_c = if_ref[...] + bcol_ref[...]
    if_r = ift_ref[...] + brow_ref[...]
    logf_c = jnp.minimum(if_c, 0.0) - _log_sigmoid_parts(if_c)
    logf_r = jnp.minimum(if_r, 0.0) - _log_sigmoid_parts(if_r)
    row = lax.broadcasted_iota(jnp.int32, (L, L), 0)
    col = lax.broadcasted_iota(jnp.int32, (L, L), 1)
    causal = col <= row
    incl_lower = jnp.where(causal, 1.0, 0.0).astype(BF16)
    incl_upper = jnp.where(row <= col, 1.0, 0.0).astype(BF16)
    b_c = sum(_dot(incl_lower, p) for p in _split_bf16(logf_c, 3))
    b_r = sum(_dot(p, incl_upper) for p in _split_bf16(logf_r, 3))

    for h in range(ML_HEADS):
        hs = slice(h * ML_HEAD_DIM, (h + 1) * ML_HEAD_DIM)
        ks = slice(D_BRANCH + h * ML_HEAD_DIM, D_BRANCH + (h + 1) * ML_HEAD_DIM)
        bcol = b_c[:, ML_HEADS + h:ML_HEADS + h + 1]
        icol = if_c[:, h:h + 1]
        brow = b_r[ML_HEADS + h:ML_HEADS + h + 1, :]
        irow = if_r[h:h + 1, :]
        m_prev = m_ref[h:h + 1, 0:1]

        d = jnp.where(causal, bcol - brow + irow, -jnp.inf)
        inter = bcol + m_prev
        m_t = jnp.maximum(inter, jnp.max(d, axis=1, keepdims=True))
        w_intra = jnp.exp(d - m_t)
        w_inter = jnp.exp(inter - m_t)

        q = qk[:, hs].astype(BF16)
        k32 = qk[:, ks] * (ML_HEAD_DIM ** -0.5)
        v = v_ref[:, hs]
        ct = ct_ref[h]
        nrow = n_ref[h:h + 1, :]
        s = _dot_nt(q, k32.astype(BF16)) * w_intra
        num = w_inter * _dot(q, ct.astype(BF16)) + _dot(s.astype(BF16), v)
        den = (w_inter * jnp.sum(q.astype(F32) * nrow, axis=1, keepdims=True)
               + jnp.sum(s, axis=1, keepdims=True))
        hid = num / jnp.maximum(jnp.abs(den), jnp.exp(-m_t))
        og = og_ref[:, hs].astype(F32)
        o_ref[:, hs] = (hid * jax.nn.sigmoid(og)).astype(o_ref.dtype)

        b_last = bcol[L - 1:L, :]
        dec = b_last - bcol + icol
        m_new = jnp.maximum(b_last + m_prev, jnp.max(dec, axis=0, keepdims=True))
        w_s = jnp.exp(dec - m_new)
        w_prev = jnp.exp(b_last + m_prev - m_new)
        kw = k32 * w_s
        ct_ref[h] = w_prev * ct + _dot_tn(kw.astype(BF16), v)
        n_ref[h:h + 1, :] = w_prev * nrow + jnp.sum(kw, axis=0, keepdims=True)
        m_ref[h:h + 1, :] = jnp.broadcast_to(m_new, (1, m_ref.shape[1]))


def _mlstm_call(main, ifg, ifg_t, conv_w, bias_col, bias_row):
    b, s, _ = main.shape
    chunk = 256
    step = chunk * ML_STEP_CHUNKS
    return pl.pallas_call(
        functools.partial(_mlstm_kernel, chunk=chunk),
        grid=(b, s // step),
        in_specs=[
            pl.BlockSpec((None, step, 2 * D_BRANCH), lambda i, j: (i, j, 2)),
            pl.BlockSpec((None, step, D_BRANCH), lambda i, j: (i, j, 6)),
            pl.BlockSpec((None, step, D_BRANCH), lambda i, j: (i, j, 7)),
            pl.BlockSpec((None, step, IF_PAD), lambda i, j: (i, j, 0)),
            pl.BlockSpec((None, 2 * ML_HEADS, step), lambda i, j: (i, 0, j)),
            pl.BlockSpec((ML_CONV, 2 * D_BRANCH), lambda i, j: (0, 0)),
            pl.BlockSpec((1, IF_PAD), lambda i, j: (0, 0)),
            pl.BlockSpec((2 * ML_HEADS, 1), lambda i, j: (0, 0)),
        ],
        out_specs=pl.BlockSpec((None, step, D_BRANCH), lambda i, j: (i, j, 0)),
        out_shape=jax.ShapeDtypeStruct((b, s, D_BRANCH), BF16),
        scratch_shapes=[
            pltpu.VMEM((chunk + CONV_HALO, 2 * D_BRANCH), F32),
            pltpu.VMEM((ML_HEADS, ML_HEAD_DIM, ML_HEAD_DIM), F32),
            pltpu.VMEM((2 * ML_HEADS, ML_HEAD_DIM), F32),
            pltpu.VMEM((2 * ML_HEADS, 128), F32),
        ],
        compiler_params=_params("parallel", "arbitrary"),
        name="mlstm",
    )(main, main, main, ifg, ifg_t, conv_w, bias_col, bias_row)


def _residual_ln(x, y, g, ln_g, ln_b, alpha):
    r = alpha * x + (1.0 + g) * y
    mu = jnp.mean(r, axis=-1, keepdims=True)
    cen = r - mu
    var = jnp.mean(cen * cen, axis=-1, keepdims=True)
    return cen * lax.rsqrt(var + LN_EPS) * ln_g + ln_b


def _mixer_out_kernel(ap_ref, pw_ref, ps_ref, ys_ref, ym_ref, g0_ref, g1_ref, g2_ref, wb_ref, wo_ref, x_ref,
                      g_ref, lng_ref, lnb_ref, o_ref, prev_ref, *, alpha):
    y_pool = _pool_tile(ap_ref, pw_ref, ps_ref, prev_ref, pl.program_id(1))
    acc = None
    for i, (y, gl_ref) in enumerate(((y_pool, g0_ref), (ys_ref[...], g1_ref), (ym_ref[...], g2_ref))):
        term = jax.nn.sigmoid(gl_ref[...].astype(F32)) * _dot(y, wb_ref[i])
        acc = term if acc is None else acc + term
    y = _dot(acc.astype(BF16), wo_ref[...])
    o_ref[...] = _residual_ln(x_ref[...], y, g_ref[...], lng_ref[...], lnb_ref[...], alpha)


def _resident_layer(stacked_shape, layer):
    tail = (0,) * (len(stacked_shape) - 1)
    return pl.BlockSpec((None,) + tuple(stacked_shape[1:]), lambda *_: (layer,) + tail,
                        pipeline_mode=pl.Buffered(1))


def _mixer_out_call(main, pool_w, pool_scale, y_sb, y_ml, w_branch, w_out, x, g, ln_g, ln_b, alpha, layer):
    b, s, d = x.shape
    tm = 256
    y_spec = pl.BlockSpec((None, tm, D_BRANCH), lambda i, j: (i, j, 0))

    def gate_spec(br):
        blk = GATE_COL // d + br
        return pl.BlockSpec((None, tm, d), lambda i, j: (i, j, blk))

    return pl.pallas_call(
        functools.partial(_mixer_out_kernel, alpha=alpha),
        grid=(b, s // tm),
        in_specs=[y_spec, _resident_layer(pool_w.shape, layer), _resident_layer(pool_scale.shape, layer),
                  y_spec, y_spec, gate_spec(0), gate_spec(1), gate_spec(2),
                  _resident_layer(w_branch.shape, layer), _resident_layer(w_out.shape, layer),
                  pl.BlockSpec((None, tm, d), lambda i, j: (i, j, 0)),
                  pl.BlockSpec((None, 1, d), lambda i, j: (i, 0, 0)),
                  pl.BlockSpec((1, d), lambda i, j: (0, 0)),
                  pl.BlockSpec((1, d), lambda i, j: (0, 0))],
        out_specs=pl.BlockSpec((None, tm, d), lambda i, j: (i, j, 0)),
        out_shape=jax.ShapeDtypeStruct((b, s, d), F32),
        scratch_shapes=[pltpu.VMEM((POOL_HALO, D_BRANCH), BF16)],
        compiler_params=_params("parallel", "arbitrary"),
        name="mixer_out_ln",
    )(main, pool_w, pool_scale, y_sb, y_ml, main, main, main, w_branch, w_out, x, g, ln_g, ln_b)


def _ffn_kernel(x_ref, xh_ref, sc_ref, sh_ref, wv_ref, wg_ref, cv_ref, cg_ref, wd_ref, g_ref,
                lng_ref, lnb_ref, o_ref, h_ref, *, alpha):
    n = pl.program_id(2)

    @pl.when(n == 0)
    def _():
        sc = 1.0 + sc_ref[...]
        sh = sh_ref[...]
        halo = jnp.where(pl.program_id(1) > 0, xh_ref[...] * sc + sh, 0.0)
        h_ref[0:ROW_HALO, :] = halo.astype(h_ref.dtype)
        h_ref[ROW_HALO:, :] = (x_ref[...] * sc + sh).astype(h_ref.dtype)
        o_ref[...] = jnp.zeros_like(o_ref)

    h = h_ref[...]

    def conv(up, c_ref, cs):
        c = c_ref[:, cs]
        out = up * c[FFN_CONV - 1:FFN_CONV, :]
        for j in range(1, FFN_CONV):
            out = out + pltpu.roll(up, j, axis=0) * c[FFN_CONV - 1 - j:FFN_CONV - j, :]
        return out[ROW_HALO:, :]

    subs = [slice(j * FFN_SUB, (j + 1) * FFN_SUB) for j in range(wv_ref.shape[1] // FFN_SUB)]
    ups = [(_dot(h, wv_ref[:, cs]), _dot(h, wg_ref[:, cs])) for cs in subs]
    part = None
    for cs, (up_val, up_gate) in zip(subs, ups):
        val = conv(up_val, cv_ref, cs)
        gate = conv(up_gate, cg_ref, cs)
        act = (gate * jax.nn.sigmoid(gate) * val).astype(BF16)
        down = _dot(act, wd_ref[cs, :])
        part = down if part is None else part + down
    o_ref[...] += part

    @pl.when(n == pl.num_programs(2) - 1)
    def _():
        o_ref[...] = _residual_ln(x_ref[...], o_ref[...], g_ref[...], lng_ref[...], lnb_ref[...], alpha)


def _ffn_call(x, sc, sh, w_up, conv_ff, w_down, g, ln_g, ln_b, alpha, layer):
    b, s, d = x.shape
    dff = w_down.shape[1]
    tm = _pick(s, (1024, 512, 256))
    tn = _pick(dff, (512, 256))
    nff = dff // tn
    halo_blocks = tm // ROW_HALO
    return pl.pallas_call(
        functools.partial(_ffn_kernel, alpha=alpha),
        grid=(b, s // tm, nff),
        in_specs=[
            pl.BlockSpec((None, tm, d), lambda i, j, k: (i, j, 0), pipeline_mode=pl.Buffered(1)),
            pl.BlockSpec((None, ROW_HALO, d), lambda i, j, k: (i, jnp.maximum(j * halo_blocks - 1, 0), 0)),
            pl.BlockSpec((None, 1, d), lambda i, j, k: (i, 0, 0)),
            pl.BlockSpec((None, 1, d), lambda i, j, k: (i, 0, 0)),
            pl.BlockSpec((None, d, tn), lambda i, j, k: (layer, 0, k)),
            pl.BlockSpec((None, d, tn), lambda i, j, k: (layer, 0, nff + k)),
            pl.BlockSpec((None, FFN_CONV, tn), lambda i, j, k: (layer, 0, k)),
            pl.BlockSpec((None, FFN_CONV, tn), lambda i, j, k: (layer, 0, nff + k)),
            pl.BlockSpec((None, tn, d), lambda i, j, k: (layer, k, 0)),
            pl.BlockSpec((None, 1, d), lambda i, j, k: (i, 0, 0)),
            pl.BlockSpec((1, d), lambda i, j, k: (0, 0)),
            pl.BlockSpec((1, d), lambda i, j, k: (0, 0)),
        ],
        out_specs=pl.BlockSpec((None, tm, d), lambda i, j, k: (i, j, 0)),
        out_shape=jax.ShapeDtypeStruct((b, s, d), F32),
        scratch_shapes=[pltpu.VMEM((tm + ROW_HALO, d), BF16)],
        compiler_params=_params("parallel", "parallel", "arbitrary", vmem_limit_bytes=FFN_VMEM_LIMIT_BYTES),
        name="ffn_conv_gate_down_ln",
    )(x, x, sc, sh, w_up, w_up, conv_ff, conv_ff, w_down, g, ln_g, ln_b)


def kernel(x, c, w_ada, b_ada, w_in, conv_ml, pool_w, pool_scale, ig_bias, fg_bias, w_branch, w_out,
           w_up, conv_ff, w_down, ln_g, ln_b):
    b, s, d = x.shape
    depth = w_in.shape[0]
    alpha = (2.0 * depth) ** 0.25

    w_main = jnp.concatenate([w_in[:, :, :IF_COL], w_in[:, :, IF_COL + 2 * ML_HEADS:]], axis=2).astype(BF16)
    w_if = w_in[:, :, IF_COL:IF_COL + IF_PAD].astype(BF16)
    pool_w, w_branch, w_out, w_up, w_down = (a.astype(BF16) for a in (pool_w, w_branch, w_out, w_up, w_down))
    pool_scale = pool_scale.reshape(depth, 1, D_BRANCH)

    mod = _mod_call(c, w_ada, b_ada)
    for l in range(depth):
        sh1, sc1, g1, sh2, sc2, g2 = (mod[l][:, None, i * d:(i + 1) * d] for i in range(6))

        main, ifg = _inproj_call(x, sc1, sh1, w_main, w_if, l)
        ifg_t = jnp.swapaxes(ifg[:, :, :2 * ML_HEADS], 1, 2)
        gate_bias = jnp.concatenate([ig_bias[l], fg_bias[l]])
        bias_col = jnp.pad(gate_bias, (0, IF_PAD - 2 * ML_HEADS)).reshape(1, IF_PAD)
        bias_row = gate_bias.reshape(2 * ML_HEADS, 1)

        y_sb = _sb_call(main)
        y_ml = _mlstm_call(main, ifg, ifg_t, conv_ml[l], bias_col, bias_row)
        x = _mixer_out_call(main, pool_w, pool_scale, y_sb, y_ml, w_branch, w_out, x, g1,
                            ln_g[l, 0].reshape(1, d), ln_b[l, 0].reshape(1, d), alpha, l)
        x = _ffn_call(x, sc2, sh2, w_up, conv_ff, w_down, g2,
                      ln_g[l, 1].reshape(1, d), ln_b[l, 1].reshape(1, d), alpha, l)
    return x
```

```python
import functools
import math

import jax
import jax.numpy as jnp
from jax import lax
from jax.experimental import pallas as pl
from jax.experimental.pallas import tpu as pltpu

F32 = jnp.float32
BF16 = jnp.bfloat16

D_BRANCH = 1024
POOL_WINDOWS = (2, 4, 8, 16)
POOL_GROUP_DIM = 256
POOL_HALO = 16
SB_HEADS = 8
SB_HEAD_DIM = 128
SB_STEP_BLOCKS = 16
ML_HEADS = 4
ML_HEAD_DIM = 256
ML_CONV = 4
ML_STEP_CHUNKS = 4
FFN_CONV = 3
FFN_SUB = 512
LN_EPS = 1e-5
GATE_COL = 8 * D_BRANCH
IF_COL = 8 * D_BRANCH
IF_PAD = 128
ROW_HALO = 16
CONV_HALO = 8

LOG2_E = 1.4426950408889634

SB_ZERO_BITS = 127.0

VMEM_LIMIT_BYTES = 48 * 1024 * 1024
INPROJ_VMEM_LIMIT_BYTES = 56 * 1024 * 1024
FFN_VMEM_LIMIT_BYTES = 58 * 1024 * 1024


def _params(*sem, vmem_limit_bytes=VMEM_LIMIT_BYTES):
    return pltpu.CompilerParams(dimension_semantics=sem, vmem_limit_bytes=vmem_limit_bytes)


def _pick(n, cands):
    for c in cands:
        if n % c == 0:
            return c
    raise ValueError(f"no tile in {cands} divides {n}")


def _dot(a, b):
    return jnp.dot(a, b, preferred_element_type=F32)


def _dot_nt(a, b):
    return lax.dot_general(a, b, (((1,), (1,)), ((), ())), preferred_element_type=F32)


def _dot_tn(a, b):
    return lax.dot_general(a, b, (((0,), (0,)), ((), ())), preferred_element_type=F32)


def _log_sigmoid_parts(z):
    return jnp.log1p(jnp.exp(-jnp.abs(z)))


def _split_bf16(a, terms):
    parts = []
    rem = a
    for _ in range(terms):
        p = rem.astype(BF16)
        parts.append(p)
        rem = rem - p.astype(F32)
    return parts


def _mod_kernel(c_ref, w_ref, b_ref, o_ref):
    c = c_ref[...]
    ca = (c * jax.nn.sigmoid(c)).astype(BF16)
    o_ref[...] = _dot(ca, w_ref[...].astype(BF16)) + b_ref[...]


def _mod_call(c, w_ada, b_ada):
    depth, d, n = w_ada.shape
    b = c.shape[0]
    tn = _pick(n, (1024, 512, 256, 128))
    return pl.pallas_call(
        _mod_kernel,
        grid=(depth, n // tn),
        in_specs=[
            pl.BlockSpec((b, d), lambda l, j: (0, 0)),
            pl.BlockSpec((None, d, tn), lambda l, j: (l, 0, j)),
            pl.BlockSpec((None, 1, tn), lambda l, j: (l, 0, j)),
        ],
        out_specs=pl.BlockSpec((None, b, tn), lambda l, j: (l, 0, j)),
        out_shape=jax.ShapeDtypeStruct((depth, b, n), F32),
        compiler_params=_params("parallel", "parallel"),
        name="adaln_mod",
    )(c, w_ada, b_ada.reshape(depth, 1, n))


def _inproj_kernel(x_ref, sc_ref, sh_ref, w_ref, wif_ref, o_ref, oif_ref, h_ref):
    @pl.when(pl.program_id(2) == 0)
    def _():
        h = (x_ref[...] * (1.0 + sc_ref[...]) + sh_ref[...]).astype(h_ref.dtype)
        h_ref[...] = h
        oif_ref[...] = _dot(h, wif_ref[...])

    o_ref[...] = _dot(h_ref[...], w_ref[...]).astype(o_ref.dtype)


def _inproj_call(x, sc, sh, w, w_if, layer):
    b, s, d = x.shape
    n = w.shape[2]
    tm = _pick(s, (1024, 512, 256))
    tn = _pick(n, (2048, 1024, 512, 256, 128))
    return pl.pallas_call(
        _inproj_kernel,
        grid=(b, s // tm, n // tn),
        in_specs=[
            pl.BlockSpec((None, tm, d), lambda i, j, k: (i, j, 0)),
            pl.BlockSpec((None, 1, d), lambda i, j, k: (i, 0, 0)),
            pl.BlockSpec((None, 1, d), lambda i, j, k: (i, 0, 0)),
            pl.BlockSpec((None, d, tn), lambda i, j, k: (layer, 0, k)),
            pl.BlockSpec((None, d, IF_PAD), lambda i, j, k: (layer, 0, 0)),
        ],
        out_specs=[pl.BlockSpec((None, tm, tn), lambda i, j, k: (i, j, k)),
                   pl.BlockSpec((None, tm, IF_PAD), lambda i, j, k: (i, j, 0))],
        out_shape=[jax.ShapeDtypeStruct((b, s, n), BF16), jax.ShapeDtypeStruct((b, s, IF_PAD), F32)],
        scratch_shapes=[pltpu.VMEM((tm, d), BF16)],
        compiler_params=_params("parallel", "parallel", "arbitrary", vmem_limit_bytes=INPROJ_VMEM_LIMIT_BYTES),
        name="in_proj",
    )(x, sc, sh, w, w_if)


def _pool_tile(a_ref, w_ref, scale_ref, prev_ref, st):
    t = a_ref.shape[0]

    @pl.when(st == 0)
    def _():
        prev_ref[...] = jnp.zeros_like(prev_ref)

    row = lax.broadcasted_iota(jnp.int32, (t, t), 0)
    col = lax.broadcasted_iota(jnp.int32, (t, t), 1)
    prow = lax.broadcasted_iota(jnp.int32, (t, POOL_HALO), 0)
    pcol = lax.broadcasted_iota(jnp.int32, (t, POOL_HALO), 1) - POOL_HALO
    t_abs = st * t + lax.broadcasted_iota(jnp.int32, (t, 1), 0)
    outs = []
    for g, win in enumerate(POOL_WINDOWS):
        cs = slice(g * POOL_GROUP_DIM, (g + 1) * POOL_GROUP_DIM)
        a = a_ref[:, cs]
        band = jnp.where((col <= row) & (col > row - win), 1.0, 0.0).astype(BF16)
        pband = jnp.where(pcol > prow - win, 1.0, 0.0).astype(BF16)
        wsum = _dot(band, a) + _dot(pband, prev_ref[:, cs])
        cnt = jnp.minimum(t_abs + 1, win).astype(F32)
        diff = (wsum / cnt - a.astype(F32)).astype(BF16)
        outs.append((_dot(diff, w_ref[g]) * scale_ref[:, cs]).astype(BF16))
    prev_ref[...] = a_ref[t - POOL_HALO:, :]
    return jnp.concatenate(outs, axis=1)


def _sb_kernel(q_ref, k_ref, v_ref, o_ref, *, tq):
    scale = SB_HEAD_DIM ** -0.5 * LOG2_E
    row = lax.broadcasted_iota(jnp.int32, (tq, tq), 0)
    col = lax.broadcasted_iota(jnp.int32, (tq, tq), 1)
    below = row > col
    after_sum = jnp.where(below, 1.0, 0.0).astype(BF16)

    def sweep(qrows, rows, accs, rs):
        diag = accs is None
        heads = [slice(h * SB_HEAD_DIM, (h + 1) * SB_HEAD_DIM) for h in range(SB_HEADS)]
        zs = [_dot_nt(q_ref[qrows, hs], k_ref[rows, hs]) * scale for hs in heads]
        log_betas, drops, his, los = [], [], [], []
        for z in zs:
            l = jnp.log2(1.0 + jnp.exp2(-jnp.abs(z)))
            drop = jnp.maximum(z, 0.0) + l
            if diag:
                drop = jnp.where(below, drop, 0.0)
            hi, lo = _split_bf16(drop, 2)
            log_betas.append(jnp.minimum(z, 0.0) - l)
            drops.append(drop)
            his.append(hi)
            los.append(lo)
        sums = _dot(jnp.concatenate(his + los, axis=0), after_sum)
        attns, new_rs, rmin = [], [], None
        for h in range(SB_HEADS):
            after = sums[h * tq:(h + 1) * tq] + sums[(SB_HEADS + h) * tq:(SB_HEADS + h + 1) * tq]
            rowsum = jnp.sum(drops[h], axis=1, keepdims=True)
            if diag:
                attn = jnp.where(below, jnp.exp2(log_betas[h] - after), 0.0)
                rn = rowsum
            else:
                attn = jnp.exp2(log_betas[h] - (after + rs[h]))
                rn = rs[h] + rowsum
            attns.append(attn.astype(BF16))
            new_rs.append(rn)
            rmin = rn if rmin is None else jnp.minimum(rmin, rn)
        pvs = [_dot(attns[h], v_ref[rows, hs]) for h, hs in enumerate(heads)]
        new_accs = pvs if diag else [a + p for a, p in zip(accs, pvs)]
        return tuple(new_accs), tuple(new_rs), jnp.min(rmin)

    def cond(carry):
        kb, rmin, _, _ = carry
        return jnp.logical_and(kb >= 0, rmin < SB_ZERO_BITS)

    step_blocks = q_ref.shape[0] // tq

    def query_block(i, carry):
        qb = pl.program_id(1) * step_blocks + i
        qrows = pl.ds(pl.multiple_of(i * tq, tq), tq)
        accs0, rs0, rmin0 = sweep(qrows, pl.ds(pl.multiple_of(qb * tq, tq), tq), None, None)

        def body(c):
            kb, _, accs, rs = c
            accs, rs, rmin = sweep(qrows, pl.ds(pl.multiple_of(kb * tq, tq), tq), accs, rs)
            return kb - 1, rmin, accs, rs

        _, _, accs, _ = lax.while_loop(cond, body, (qb - 1, rmin0, accs0, rs0))
        for h in range(SB_HEADS):
            o_ref[qrows, h * SB_HEAD_DIM:(h + 1) * SB_HEAD_DIM] = accs[h].astype(o_ref.dtype)
        return carry

    lax.fori_loop(0, step_blocks, query_block, 0)


def _sb_call(main):
    b, s, _ = main.shape
    tq = 128
    step = tq * math.gcd(SB_STEP_BLOCKS, s // tq)
    return pl.pallas_call(
        functools.partial(_sb_kernel, tq=tq),
        grid=(b, s // step),
        in_specs=[
            pl.BlockSpec((None, step, D_BRANCH), lambda i, j: (i, j, 1)),
            pl.BlockSpec((None, s, D_BRANCH), lambda i, j: (i, 0, 2)),
            pl.BlockSpec((None, s, D_BRANCH), lambda i, j: (i, 0, 3)),
        ],
        out_specs=pl.BlockSpec((None, step, D_BRANCH), lambda i, j: (i, j, 0)),
        out_shape=jax.ShapeDtypeStruct((b, s, D_BRANCH), BF16),
        compiler_params=_params("parallel", "arbitrary"),
        name="stick_breaking",
    )(main, main, main)


def _mlstm_kernel(qk_ref, v_ref, og_ref, if_ref, ift_ref, cw_ref, bcol_ref, brow_ref, o_ref,
                  ext_ref, ct_ref, n_ref, m_ref, *, chunk):
    @pl.when(pl.program_id(1) == 0)
    def _():
        ext_ref[0:CONV_HALO, :] = jnp.zeros((CONV_HALO, ext_ref.shape[1]), F32)
        ct_ref[...] = jnp.zeros_like(ct_ref)
        n_ref[...] = jnp.zeros_like(n_ref)
        m_ref[...] = jnp.zeros_like(m_ref)

    for c in range(qk_ref.shape[0] // chunk):
        rows = slice(c * chunk, (c + 1) * chunk)
        _mlstm_chunk(qk_ref.at[rows, :], v_ref.at[rows, :], og_ref.at[rows, :], if_ref.at[rows, :],
                     ift_ref.at[:, rows], cw_ref, bcol_ref, brow_ref, o_ref.at[rows, :],
                     ext_ref, ct_ref, n_ref, m_ref, chunk)


def _mlstm_chunk(qk_ref, v_ref, og_ref, if_ref, ift_ref, cw_ref, bcol_ref, brow_ref, o_ref,
                 ext_ref, ct_ref, n_ref, m_ref, chunk):
    L = chunk

    ext_ref[CONV_HALO:, :] = qk_ref[...].astype(F32)
    ext = ext_ref[...]
    cw = cw_ref[...]
    conv = ext * cw[ML_CONV - 1:ML_CONV, :]
    for j in range(1, ML_CONV):
        conv = conv + pltpu.roll(ext, j, axis=0) * cw[ML_CONV - 1 - j:ML_CONV - j, :]
    conv = conv[CONV_HALO:, :]
    qk = conv * jax.nn.sigmoid(conv)
    ext_ref[0:CONV_HALO, :] = ext_ref[L:L + CONV_HALO, :]

    if_c = if_ref[...] + bcol_ref[...]
    if_r = ift_ref[...] + brow_ref[...]
    logf_c = jnp.minimum(if_c, 0.0) - _log_sigmoid_parts(if_c)
    logf_r = jnp.minimum(if_r, 0.0) - _log_sigmoid_parts(if_r)
    row = lax.broadcasted_iota(jnp.int32, (L, L), 0)
    col = lax.broadcasted_iota(jnp.int32, (L, L), 1)
    causal = col <= row
    incl_lower = jnp.where(causal, 1.0, 0.0).astype(BF16)
    incl_upper = jnp.where(row <= col, 1.0, 0.0).astype(BF16)
    b_c = sum(_dot(incl_lower, p) for p in _split_bf16(logf_c, 3))
    b_r = sum(_dot(p, incl_upper) for p in _split_bf16(logf_r, 3))

    for h in range(ML_HEADS):
        hs = slice(h * ML_HEAD_DIM, (h + 1) * ML_HEAD_DIM)
        ks = slice(D_BRANCH + h * ML_HEAD_DIM, D_BRANCH + (h + 1) * ML_HEAD_DIM)
        bcol = b_c[:, ML_HEADS + h:ML_HEADS + h + 1]
        icol = if_c[:, h:h + 1]
        brow = b_r[ML_HEADS + h:ML_HEADS + h + 1, :]
        irow = if_r[h:h + 1, :]
        m_prev = m_ref[h:h + 1, 0:1]

        d = jnp.where(causal, bcol - brow + irow, -jnp.inf)
        inter = bcol + m_prev
        m_t = jnp.maximum(inter, jnp.max(d, axis=1, keepdims=True))
        w_intra = jnp.exp(d - m_t)
        w_inter = jnp.exp(inter - m_t)

        q = qk[:, hs].astype(BF16)
        k32 = qk[:, ks] * (ML_HEAD_DIM ** -0.5)
        v = v_ref[:, hs]
        ct = ct_ref[h]
        nrow = n_ref[h:h + 1, :]
        s = _dot_nt(q, k32.astype(BF16)) * w_intra
        num = w_inter * _dot(q, ct.astype(BF16)) + _dot(s.astype(BF16), v)
        den = (w_inter * jnp.sum(q.astype(F32) * nrow, axis=1, keepdims=True)
               + jnp.sum(s, axis=1, keepdims=True))
        hid = num / jnp.maximum(jnp.abs(den), jnp.exp(-m_t))
        og = og_ref[:, hs].astype(F32)
        o_ref[:, hs] = (hid * jax.nn.sigmoid(og)).astype(o_ref.dtype)

        b_last = bcol[L - 1:L, :]
        dec = b_last - bcol + icol
        m_new = jnp.maximum(b_last + m_prev, jnp.max(dec, axis=0, keepdims=True))
        w_s = jnp.exp(dec - m_new)
        w_prev = jnp.exp(b_last + m_prev - m_new)
        kw = k32 * w_s
        ct_ref[h] = w_prev * ct + _dot_tn(kw.astype(BF16), v)
        n_ref[h:h + 1, :] = w_prev * nrow + jnp.sum(kw, axis=0, keepdims=True)
        m_ref[h:h + 1, :] = jnp.broadcast_to(m_new, (1, m_ref.shape[1]))


def _mlstm_call(main, ifg, ifg_t, conv_w, bias_col, bias_row):
    b, s, _ = main.shape
    chunk = 256
    step = chunk * math.gcd(ML_STEP_CHUNKS, s // chunk)
    return pl.pallas_call(
        functools.partial(_mlstm_kernel, chunk=chunk),
        grid=(b, s // step),
        in_specs=[
            pl.BlockSpec((None, step, 2 * D_BRANCH), lambda i, j: (i, j, 2)),
            pl.BlockSpec((None, step, D_BRANCH), lambda i, j: (i, j, 6)),
            pl.BlockSpec((None, step, D_BRANCH), lambda i, j: (i, j, 7)),
            pl.BlockSpec((None, step, IF_PAD), lambda i, j: (i, j, 0)),
            pl.BlockSpec((None, 2 * ML_HEADS, step), lambda i, j: (i, 0, j)),
            pl.BlockSpec((ML_CONV, 2 * D_BRANCH), lambda i, j: (0, 0)),
            pl.BlockSpec((1, IF_PAD), lambda i, j: (0, 0)),
            pl.BlockSpec((2 * ML_HEADS, 1), lambda i, j: (0, 0)),
        ],
        out_specs=pl.BlockSpec((None, step, D_BRANCH), lambda i, j: (i, j, 0)),
        out_shape=jax.ShapeDtypeStruct((b, s, D_BRANCH), BF16),
        scratch_shapes=[
            pltpu.VMEM((chunk + CONV_HALO, 2 * D_BRANCH), F32),
            pltpu.VMEM((ML_HEADS, ML_HEAD_DIM, ML_HEAD_DIM), F32),
            pltpu.VMEM((2 * ML_HEADS, ML_HEAD_DIM), F32),
            pltpu.VMEM((2 * ML_HEADS, 128), F32),
        ],
        compiler_params=_params("parallel", "arbitrary"),
        name="mlstm",
    )(main, main, main, ifg, ifg_t, conv_w, bias_col, bias_row)


def _residual_ln(x, y, g, ln_g, ln_b, alpha):
    r = alpha * x + (1.0 + g) * y
    mu = jnp.mean(r, axis=-1, keepdims=True)
    cen = r - mu
    var = jnp.mean(cen * cen, axis=-1, keepdims=True)
    return cen * lax.rsqrt(var + LN_EPS) * ln_g + ln_b


def _mixer_out_kernel(ap_ref, pw_ref, ps_ref, ys_ref, ym_ref, g0_ref, g1_ref, g2_ref, wb_ref, wo_ref, x_ref,
                      g_ref, lng_ref, lnb_ref, o_ref, prev_ref, *, alpha):
    y_pool = _pool_tile(ap_ref, pw_ref, ps_ref, prev_ref, pl.program_id(1))
    acc = None
    for i, (y, gl_ref) in enumerate(((y_pool, g0_ref), (ys_ref[...], g1_ref), (ym_ref[...], g2_ref))):
        term = jax.nn.sigmoid(gl_ref[...].astype(F32)) * _dot(y, wb_ref[i])
        acc = term if acc is None else acc + term
    y = _dot(acc.astype(BF16), wo_ref[...])
    o_ref[...] = _residual_ln(x_ref[...], y, g_ref[...], lng_ref[...], lnb_ref[...], alpha)


def _resident_layer(stacked_shape, layer):
    tail = (0,) * (len(stacked_shape) - 1)
    return pl.BlockSpec((None,) + tuple(stacked_shape[1:]), lambda *_: (layer,) + tail,
                        pipeline_mode=pl.Buffered(1))


def _mixer_out_call(main, pool_w, pool_scale, y_sb, y_ml, w_branch, w_out, x, g, ln_g, ln_b, alpha, layer):
    b, s, d = x.shape
    tm = 256
    y_spec = pl.BlockSpec((None, tm, D_BRANCH), lambda i, j: (i, j, 0))

    def gate_spec(br):
        blk = GATE_COL // d + br
        return pl.BlockSpec((None, tm, d), lambda i, j: (i, j, blk))

    return pl.pallas_call(
        functools.partial(_mixer_out_kernel, alpha=alpha),
        grid=(b, s // tm),
        in_specs=[y_spec, _resident_layer(pool_w.shape, layer), _resident_layer(pool_scale.shape, layer),
                  y_spec, y_spec, gate_spec(0), gate_spec(1), gate_spec(2),
                  _resident_layer(w_branch.shape, layer), _resident_layer(w_out.shape, layer),
                  pl.BlockSpec((None, tm, d), lambda i, j: (i, j, 0)),
                  pl.BlockSpec((None, 1, d), lambda i, j: (i, 0, 0)),
                  pl.BlockSpec((1, d), lambda i, j: (0, 0)),
                  pl.BlockSpec((1, d), lambda i, j: (0, 0))],
        out_specs=pl.BlockSpec((None, tm, d), lambda i, j: (i, j, 0)),
        out_shape=jax.ShapeDtypeStruct((b, s, d), F32),
        scratch_shapes=[pltpu.VMEM((POOL_HALO, D_BRANCH), BF16)],
        compiler_params=_params("parallel", "arbitrary"),
        name="mixer_out_ln",
    )(main, pool_w, pool_scale, y_sb, y_ml, main, main, main, w_branch, w_out, x, g, ln_g, ln_b)


def _ffn_kernel(x_ref, xh_ref, sc_ref, sh_ref, wv_ref, wg_ref, cv_ref, cg_ref, wd_ref, g_ref,
                lng_ref, lnb_ref, o_ref, h_ref, *, alpha):
    n = pl.program_id(2)

    @pl.when(n == 0)
    def _():
        sc = 1.0 + sc_ref[...]
        sh = sh_ref[...]
        halo = jnp.where(pl.program_id(1) > 0, xh_ref[...] * sc + sh, 0.0)
        h_ref[0:ROW_HALO, :] = halo.astype(h_ref.dtype)
        h_ref[ROW_HALO:, :] = (x_ref[...] * sc + sh).astype(h_ref.dtype)
        o_ref[...] = jnp.zeros_like(o_ref)

    h = h_ref[...]

    def conv(up, c_ref, cs):
        c = c_ref[:, cs]
        out = up * c[FFN_CONV - 1:FFN_CONV, :]
        for j in range(1, FFN_CONV):
            out = out + pltpu.roll(up, j, axis=0) * c[FFN_CONV - 1 - j:FFN_CONV - j, :]
        return out[ROW_HALO:, :]

    subs = [slice(j * FFN_SUB, (j + 1) * FFN_SUB) for j in range(wv_ref.shape[1] // FFN_SUB)]
    ups = [(_dot(h, wv_ref[:, cs]), _dot(h, wg_ref[:, cs])) for cs in subs]
    part = None
    for cs, (up_val, up_gate) in zip(subs, ups):
        val = conv(up_val, cv_ref, cs)
        gate = conv(up_gate, cg_ref, cs)
        act = (gate * jax.nn.sigmoid(gate) * val).astype(BF16)
        down = _dot(act, wd_ref[cs, :])
        part = down if part is None else part + down
    o_ref[...] += part

    @pl.when(n == pl.num_programs(2) - 1)
    def _():
        o_ref[...] = _residual_ln(x_ref[...], o_ref[...], g_ref[...], lng_ref[...], lnb_ref[...], alpha)


def _ffn_call(x, sc, sh, w_up, conv_ff, w_down, g, ln_g, ln_b, alpha, layer):
    b, s, d = x.shape
    dff = w_down.shape[1]
    tm = _pick(s, (1024, 512, 256))
    tn = _pick(dff, (512, 256))
    nff = dff // tn
    halo_blocks = tm // ROW_HALO
    return pl.pallas_call(
        functools.partial(_ffn_kernel, alpha=alpha),
        grid=(b, s // tm, nff),
        in_specs=[
            pl.BlockSpec((None, tm, d), lambda i, j, k: (i, j, 0), pipeline_mode=pl.Buffered(1)),
            pl.BlockSpec((None, ROW_HALO, d), lambda i, j, k: (i, jnp.maximum(j * halo_blocks - 1, 0), 0)),
            pl.BlockSpec((None, 1, d), lambda i, j, k: (i, 0, 0)),
            pl.BlockSpec((None, 1, d), lambda i, j, k: (i, 0, 0)),
            pl.BlockSpec((None, d, tn), lambda i, j, k: (layer, 0, k)),
            pl.BlockSpec((None, d, tn), lambda i, j, k: (layer, 0, nff + k)),
            pl.BlockSpec((None, FFN_CONV, tn), lambda i, j, k: (layer, 0, k)),
            pl.BlockSpec((None, FFN_CONV, tn), lambda i, j, k: (layer, 0, nff + k)),
            pl.BlockSpec((None, tn, d), lambda i, j, k: (layer, k, 0)),
            pl.BlockSpec((None, 1, d), lambda i, j, k: (i, 0, 0)),
            pl.BlockSpec((1, d), lambda i, j, k: (0, 0)),
            pl.BlockSpec((1, d), lambda i, j, k: (0, 0)),
        ],
        out_specs=pl.BlockSpec((None, tm, d), lambda i, j, k: (i, j, 0)),
        out_shape=jax.ShapeDtypeStruct((b, s, d), F32),
        scratch_shapes=[pltpu.VMEM((tm + ROW_HALO, d), BF16)],
        compiler_params=_params("parallel", "parallel", "arbitrary", vmem_limit_bytes=FFN_VMEM_LIMIT_BYTES),
        name="ffn_conv_gate_down_ln",
    )(x, x, sc, sh, w_up, w_up, conv_ff, conv_ff, w_down, g, ln_g, ln_b)


def kernel(x, c, w_ada, b_ada, w_in, conv_ml, pool_w, pool_scale, ig_bias, fg_bias, w_branch, w_out,
           w_up, conv_ff, w_down, ln_g, ln_b):
    b, s, d = x.shape
    depth = w_in.shape[0]
    alpha = (2.0 * depth) ** 0.25

    w_main = jnp.concatenate([w_in[:, :, :IF_COL], w_in[:, :, IF_COL + 2 * ML_HEADS:]], axis=2).astype(BF16)
    w_if = w_in[:, :, IF_COL:IF_COL + IF_PAD].astype(BF16)
    pool_w, w_branch, w_out, w_up, w_down = (a.astype(BF16) for a in (pool_w, w_branch, w_out, w_up, w_down))
    pool_scale = pool_scale.reshape(depth, 1, D_BRANCH)

    mod = _mod_call(c, w_ada, b_ada)
    for l in range(depth):
        sh1, sc1, g1, sh2, sc2, g2 = (mod[l][:, None, i * d:(i + 1) * d] for i in range(6))

        main, ifg = _inproj_call(x, sc1, sh1, w_main, w_if, l)
        ifg_t = jnp.swapaxes(ifg[:, :, :2 * ML_HEADS], 1, 2)
        gate_bias = jnp.concatenate([ig_bias[l], fg_bias[l]])
        bias_col = jnp.pad(gate_bias, (0, IF_PAD - 2 * ML_HEADS)).reshape(1, IF_PAD)
        bias_row = gate_bias.reshape(2 * ML_HEADS, 1)

        y_sb = _sb_call(main)
        y_ml = _mlstm_call(main, ifg, ifg_t, conv_ml[l], bias_col, bias_row)
        x = _mixer_out_call(main, pool_w, pool_scale, y_sb, y_ml, w_branch, w_out, x, g1,
                            ln_g[l, 0].reshape(1, d), ln_b[l, 0].reshape(1, d), alpha, l)
        x = _ffn_call(x, sc2, sh2, w_up, conv_ff, w_down, g2,
                      ln_g[l, 1].reshape(1, d), ln_b[l, 1].reshape(1, d), alpha, l)
    return x
```

```python
import functools

import jax
import jax.numpy as jnp
from jax import lax
from jax.experimental import pallas as pl
from jax.experimental.pallas import tpu as pltpu

F32 = jnp.float32
BF16 = jnp.bfloat16

D_BRANCH = 1024
POOL_WINDOWS = (2, 4, 8, 16)
POOL_GROUP_DIM = 256
POOL_HALO = 16
SB_HEADS = 8
SB_HEAD_DIM = 128
SB_STEP_BLOCKS = 4
ML_HEADS = 4
ML_HEAD_DIM = 256
ML_CONV = 4
ML_STEP_CHUNKS = 2
FFN_CONV = 3
FFN_SUB = 512
LN_EPS = 1e-5
GATE_COL = 8 * D_BRANCH
IF_COL = 8 * D_BRANCH
IF_PAD = 128
ROW_HALO = 16
CONV_HALO = 8

LOG2_E = 1.4426950408889634

SB_ZERO_BITS = 127.0

VMEM_LIMIT_BYTES = 48 * 1024 * 1024
INPROJ_VMEM_LIMIT_BYTES = 56 * 1024 * 1024
FFN_VMEM_LIMIT_BYTES = 58 * 1024 * 1024


def _params(*sem, vmem_limit_bytes=VMEM_LIMIT_BYTES):
    return pltpu.CompilerParams(dimension_semantics=sem, vmem_limit_bytes=vmem_limit_bytes)


def _pick(n, cands):
    for c in cands:
        if n % c == 0:
            return c
    raise ValueError(f"no tile in {cands} divides {n}")


def _dot(a, b):
    return jnp.dot(a, b, preferred_element_type=F32)


def _dot_nt(a, b):
    return lax.dot_general(a, b, (((1,), (1,)), ((), ())), preferred_element_type=F32)


def _dot_tn(a, b):
    return lax.dot_general(a, b, (((0,), (0,)), ((), ())), preferred_element_type=F32)


def _log_sigmoid_parts(z):
    return jnp.log1p(jnp.exp(-jnp.abs(z)))


def _split_bf16(a, terms):
    parts = []
    rem = a
    for _ in range(terms):
        p = rem.astype(BF16)
        parts.append(p)
        rem = rem - p.astype(F32)
    return parts


def _mod_kernel(c_ref, w_ref, b_ref, o_ref):
    c = c_ref[...]
    ca = (c * jax.nn.sigmoid(c)).astype(BF16)
    o_ref[...] = _dot(ca, w_ref[...].astype(BF16)) + b_ref[...]


def _mod_call(c, w_ada, b_ada):
    depth, d, n = w_ada.shape
    b = c.shape[0]
    tn = _pick(n, (1024, 512, 256, 128))
    return pl.pallas_call(
        _mod_kernel,
        grid=(depth, n // tn),
        in_specs=[
            pl.BlockSpec((b, d), lambda l, j: (0, 0)),
            pl.BlockSpec((None, d, tn), lambda l, j: (l, 0, j)),
            pl.BlockSpec((None, 1, tn), lambda l, j: (l, 0, j)),
        ],
        out_specs=pl.BlockSpec((None, b, tn), lambda l, j: (l, 0, j)),
        out_shape=jax.ShapeDtypeStruct((depth, b, n), F32),
        compiler_params=_params("parallel", "parallel"),
        name="adaln_mod",
    )(c, w_ada, b_ada.reshape(depth, 1, n))


def _inproj_kernel(x_ref, sc_ref, sh_ref, w_ref, wif_ref, o_ref, oif_ref, h_ref):
    @pl.when(pl.program_id(2) == 0)
    def _():
        h = (x_ref[...] * (1.0 + sc_ref[...]) + sh_ref[...]).astype(h_ref.dtype)
        h_ref[...] = h
        oif_ref[...] = _dot(h, wif_ref[...])

    o_ref[...] = _dot(h_ref[...], w_ref[...]).astype(o_ref.dtype)


def _inproj_call(x, sc, sh, w, w_if, layer):
    b, s, d = x.shape
    n = w.shape[2]
    tm = _pick(s, (1024, 512, 256))
    tn = _pick(n, (2048, 1024, 512, 256, 128))
    return pl.pallas_call(
        _inproj_kernel,
        grid=(b, s // tm, n // tn),
        in_specs=[
            pl.BlockSpec((None, tm, d), lambda i, j, k: (i, j, 0)),
            pl.BlockSpec((None, 1, d), lambda i, j, k: (i, 0, 0)),
            pl.BlockSpec((None, 1, d), lambda i, j, k: (i, 0, 0)),
            pl.BlockSpec((None, d, tn), lambda i, j, k: (layer, 0, k)),
            pl.BlockSpec((None, d, IF_PAD), lambda i, j, k: (layer, 0, 0)),
        ],
        out_specs=[pl.BlockSpec((None, tm, tn), lambda i, j, k: (i, j, k)),
                   pl.BlockSpec((None, tm, IF_PAD), lambda i, j, k: (i, j, 0))],
        out_shape=[jax.ShapeDtypeStruct((b, s, n), BF16), jax.ShapeDtypeStruct((b, s, IF_PAD), F32)],
        scratch_shapes=[pltpu.VMEM((tm, d), BF16)],
        compiler_params=_params("parallel", "parallel", "arbitrary", vmem_limit_bytes=INPROJ_VMEM_LIMIT_BYTES),
        name="in_proj",
    )(x, sc, sh, w, w_if)


def _pool_tile(a_ref, w_ref, scale_ref, prev_ref, st):
    t = a_ref.shape[0]

    @pl.when(st == 0)
    def _():
        prev_ref[...] = jnp.zeros_like(prev_ref)

    row = lax.broadcasted_iota(jnp.int32, (t, t), 0)
    col = lax.broadcasted_iota(jnp.int32, (t, t), 1)
    prow = lax.broadcasted_iota(jnp.int32, (t, POOL_HALO), 0)
    pcol = lax.broadcasted_iota(jnp.int32, (t, POOL_HALO), 1) - POOL_HALO
    t_abs = st * t + lax.broadcasted_iota(jnp.int32, (t, 1), 0)
    outs = []
    for g, win in enumerate(POOL_WINDOWS):
        cs = slice(g * POOL_GROUP_DIM, (g + 1) * POOL_GROUP_DIM)
        a = a_ref[:, cs]
        band = jnp.where((col <= row) & (col > row - win), 1.0, 0.0).astype(BF16)
        pband = jnp.where(pcol > prow - win, 1.0, 0.0).astype(BF16)
        wsum = _dot(band, a) + _dot(pband, prev_ref[:, cs])
        cnt = jnp.minimum(t_abs + 1, win).astype(F32)
        diff = (wsum / cnt - a.astype(F32)).astype(BF16)
        outs.append((_dot(diff, w_ref[g]) * scale_ref[:, cs]).astype(BF16))
    prev_ref[...] = a_ref[t - POOL_HALO:, :]
    return jnp.concatenate(outs, axis=1)


def _sb_kernel(q_ref, k_ref, v_ref, o_ref, *, tq):
    scale = SB_HEAD_DIM ** -0.5 * LOG2_E
    row = lax.broadcasted_iota(jnp.int32, (tq, tq), 0)
    col = lax.broadcasted_iota(jnp.int32, (tq, tq), 1)
    below = row > col
    after_sum = jnp.where(below, 1.0, 0.0).astype(BF16)

    def sweep(qrows, rows, accs, rs):
        diag = accs is None
        heads = [slice(h * SB_HEAD_DIM, (h + 1) * SB_HEAD_DIM) for h in range(SB_HEADS)]
        zs = [_dot_nt(q_ref[qrows, hs], k_ref[rows, hs]) * scale for hs in heads]
        log_betas, drops, his, los = [], [], [], []
        for z in zs:
            l = jnp.log2(1.0 + jnp.exp2(-jnp.abs(z)))
            drop = jnp.maximum(z, 0.0) + l
            if diag:
                drop = jnp.where(below, drop, 0.0)
            hi, lo = _split_bf16(drop, 2)
            log_betas.append(jnp.minimum(z, 0.0) - l)
            drops.append(drop)
            his.append(hi)
            los.append(lo)
        sums = _dot(jnp.concatenate(his + los, axis=0), after_sum)
        attns, new_rs, rmin = [], [], None
        for h in range(SB_HEADS):
            after = sums[h * tq:(h + 1) * tq] + sums[(SB_HEADS + h) * tq:(SB_HEADS + h + 1) * tq]
            rowsum = jnp.sum(drops[h], axis=1, keepdims=True)
            if diag:
                attn = jnp.where(below, jnp.exp2(log_betas[h] - after), 0.0)
                rn = rowsum
            else:
                attn = jnp.exp2(log_betas[h] - (after + rs[h]))
                rn = rs[h] + rowsum
            attns.append(attn.astype(BF16))
            new_rs.append(rn)
            rmin = rn if rmin is None else jnp.minimum(rmin, rn)
        pvs = [_dot(attns[h], v_ref[rows, hs]) for h, hs in enumerate(heads)]
        new_accs = pvs if diag else [a + p for a, p in zip(accs, pvs)]
        return tuple(new_accs), tuple(new_rs), jnp.min(rmin)

    def cond(carry):
        kb, rmin, _, _ = carry
        return jnp.logical_and(kb >= 0, rmin < SB_ZERO_BITS)

    def query_block(i, carry):
        qb = pl.program_id(1) * SB_STEP_BLOCKS + i
        qrows = pl.ds(pl.multiple_of(i * tq, tq), tq)
        accs0, rs0, rmin0 = sweep(qrows, pl.ds(pl.multiple_of(qb * tq, tq), tq), None, None)

        def body(c):
            kb, _, accs, rs = c
            accs, rs, rmin = sweep(qrows, pl.ds(pl.multiple_of(kb * tq, tq), tq), accs, rs)
            return kb - 1, rmin, accs, rs

        _, _, accs, _ = lax.while_loop(cond, body, (qb - 1, rmin0, accs0, rs0))
        for h in range(SB_HEADS):
            o_ref[qrows, h * SB_HEAD_DIM:(h + 1) * SB_HEAD_DIM] = accs[h].astype(o_ref.dtype)
        return carry

    lax.fori_loop(0, SB_STEP_BLOCKS, query_block, 0)


def _sb_call(main):
    b, s, _ = main.shape
    tq = 128
    step = tq * SB_STEP_BLOCKS
    return pl.pallas_call(
        functools.partial(_sb_kernel, tq=tq),
        grid=(b, s // step),
        in_specs=[
            pl.BlockSpec((None, step, D_BRANCH), lambda i, j: (i, j, 1)),
            pl.BlockSpec((None, s, D_BRANCH), lambda i, j: (i, 0, 2)),
            pl.BlockSpec((None, s, D_BRANCH), lambda i, j: (i, 0, 3)),
        ],
        out_specs=pl.BlockSpec((None, step, D_BRANCH), lambda i, j: (i, j, 0)),
        out_shape=jax.ShapeDtypeStruct((b, s, D_BRANCH), BF16),
        compiler_params=_params("parallel", "arbitrary"),
        name="stick_breaking",
    )(main, main, main)


def _mlstm_kernel(qk_ref, v_ref, og_ref, if_ref, ift_ref, cw_ref, bcol_ref, brow_ref, o_ref,
                  ext_ref, ct_ref, n_ref, m_ref, *, chunk):
    @pl.when(pl.program_id(1) == 0)
    def _():
        ext_ref[0:CONV_HALO, :] = jnp.zeros((CONV_HALO, ext_ref.shape[1]), F32)
        ct_ref[...] = jnp.zeros_like(ct_ref)
        n_ref[...] = jnp.zeros_like(n_ref)
        m_ref[...] = jnp.zeros_like(m_ref)

    for c in range(ML_STEP_CHUNKS):
        rows = slice(c * chunk, (c + 1) * chunk)
        _mlstm_chunk(qk_ref.at[rows, :], v_ref.at[rows, :], og_ref.at[rows, :], if_ref.at[rows, :],
                     ift_ref.at[:, rows], cw_ref, bcol_ref, brow_ref, o_ref.at[rows, :],
                     ext_ref, ct_ref, n_ref, m_ref, chunk)


def _mlstm_chunk(qk_ref, v_ref, og_ref, if_ref, ift_ref, cw_ref, bcol_ref, brow_ref, o_ref,
                 ext_ref, ct_ref, n_ref, m_ref, chunk):
    L = chunk

    ext_ref[CONV_HALO:, :] = qk_ref[...].astype(F32)
    ext = ext_ref[...]
    cw = cw_ref[...]
    conv = ext * cw[ML_CONV - 1:ML_CONV, :]
    for j in range(1, ML_CONV):
        conv = conv + pltpu.roll(ext, j, axis=0) * cw[ML_CONV - 1 - j:ML_CONV - j, :]
    conv = conv[CONV_HALO:, :]
    qk = conv * jax.nn.sigmoid(conv)
    ext_ref[0:CONV_HALO, :] = ext_ref[L:L + CONV_HALO, :]

    if_c = if_ref[...] + bcol_ref[...]
    if_r = ift_ref[...] + brow_ref[...]
    logf_c = jnp.minimum(if_c, 0.0) - _log_sigmoid_parts(if_c)
    logf_r = jnp.minimum(if_r, 0.0) - _log_sigmoid_parts(if_r)
    row = lax.broadcasted_iota(jnp.int32, (L, L), 0)
    col = lax.broadcasted_iota(jnp.int32, (L, L), 1)
    causal = col <= row
    incl_lower = jnp.where(causal, 1.0, 0.0).astype(BF16)
    incl_upper = jnp.where(row <= col, 1.0, 0.0).astype(BF16)
    b_c = sum(_dot(incl_lower, p) for p in _split_bf16(logf_c, 3))
    b_r = sum(_dot(p, incl_upper) for p in _split_bf16(logf_r, 3))

    for h in range(ML_HEADS):
        hs = slice(h * ML_HEAD_DIM, (h + 1) * ML_HEAD_DIM)
        ks = slice(D_BRANCH + h * ML_HEAD_DIM, D_BRANCH + (h + 1) * ML_HEAD_DIM)
        bcol = b_c[:, ML_HEADS + h:ML_HEADS + h + 1]
        icol = if_c[:, h:h + 1]
        brow = b_r[ML_HEADS + h:ML_HEADS + h + 1, :]
        irow = if_r[h:h + 1, :]
        m_prev = m_ref[h:h + 1, 0:1]

        d = jnp.where(causal, bcol - brow + irow, -jnp.inf)
        inter = bcol + m_prev
        m_t = jnp.maximum(inter, jnp.max(d, axis=1, keepdims=True))
        w_intra = jnp.exp(d - m_t)
        w_inter = jnp.exp(inter - m_t)

        q = qk[:, hs].astype(BF16)
        k32 = qk[:, ks] * (ML_HEAD_DIM ** -0.5)
        v = v_ref[:, hs]
        ct = ct_ref[h]
        nrow = n_ref[h:h + 1, :]
        s = _dot_nt(q, k32.astype(BF16)) * w_intra
        num = w_inter * _dot(q, ct.astype(BF16)) + _dot(s.astype(BF16), v)
        den = (w_inter * jnp.sum(q.astype(F32) * nrow, axis=1, keepdims=True)
               + jnp.sum(s, axis=1, keepdims=True))
        hid = num / jnp.maximum(jnp.abs(den), jnp.exp(-m_t))
        og = og_ref[:, hs].astype(F32)
        o_ref[:, hs] = (hid * jax.nn.sigmoid(og)).astype(o_ref.dtype)

        b_last = bcol[L - 1:L, :]
        dec = b_last - bcol + icol
        m_new = jnp.maximum(b_last + m_prev, jnp.max(dec, axis=0, keepdims=True))
        w_s = jnp.exp(dec - m_new)
        w_prev = jnp.exp(b_last + m_prev - m_new)
        kw = k32 * w_s
        ct_ref[h] = w_prev * ct + _dot_tn(kw.astype(BF16), v)
        n_ref[h:h + 1, :] = w_prev * nrow + jnp.sum(kw, axis=0, keepdims=True)
        m_ref[h:h + 1, :] = jnp.broadcast_to(m_new, (1, m_ref.shape[1]))


def _mlstm_call(main, ifg, ifg_t, conv_w, bias_col, bias_row):
    b, s, _ = main.shape
    chunk = 256
    step = chunk * ML_STEP_CHUNKS
    return pl.pallas_call(
        functools.partial(_mlstm_kernel, chunk=chunk),
        grid=(b, s // step),
        in_specs=[
            pl.BlockSpec((None, step, 2 * D_BRANCH), lambda i, j: (i, j, 2)),
            pl.BlockSpec((None, step, D_BRANCH), lambda i, j: (i, j, 6)),
            pl.BlockSpec((None, step, D_BRANCH), lambda i, j: (i, j, 7)),
            pl.BlockSpec((None, step, IF_PAD), lambda i, j: (i, j, 0)),
            pl.BlockSpec((None, 2 * ML_HEADS, step), lambda i, j: (i, 0, j)),
            pl.BlockSpec((ML_CONV, 2 * D_BRANCH), lambda i, j: (0, 0)),
            pl.BlockSpec((1, IF_PAD), lambda i, j: (0, 0)),
            pl.BlockSpec((2 * ML_HEADS, 1), lambda i, j: (0, 0)),
        ],
        out_specs=pl.BlockSpec((None, step, D_BRANCH), lambda i, j: (i, j, 0)),
        out_shape=jax.ShapeDtypeStruct((b, s, D_BRANCH), BF16),
        scratch_shapes=[
            pltpu.VMEM((chunk + CONV_HALO, 2 * D_BRANCH), F32),
            pltpu.VMEM((ML_HEADS, ML_HEAD_DIM, ML_HEAD_DIM), F32),
            pltpu.VMEM((2 * ML_HEADS, ML_HEAD_DIM), F32),
            pltpu.VMEM((2 * ML_HEADS, 128), F32),
        ],
        compiler_params=_params("parallel", "arbitrary"),
        name="mlstm",
    )(main, main, main, ifg, ifg_t, conv_w, bias_col, bias_row)


def _residual_ln(x, y, g, ln_g, ln_b, alpha):
    r = alpha * x + (1.0 + g) * y
    mu = jnp.mean(r, axis=-1, keepdims=True)
    cen = r - mu
    var = jnp.mean(cen * cen, axis=-1, keepdims=True)
    return cen * lax.rsqrt(var + LN_EPS) * ln_g + ln_b


def _mixer_out_kernel(ap_ref, pw_ref, ps_ref, ys_ref, ym_ref, g0_ref, g1_ref, g2_ref, wb_ref, wo_ref, x_ref,
                      g_ref, lng_ref, lnb_ref, o_ref, prev_ref, *, alpha):
    y_pool = _pool_tile(ap_ref, pw_ref, ps_ref, prev_ref, pl.program_id(1))
    acc = None
    for i, (y, gl_ref) in enumerate(((y_pool, g0_ref), (ys_ref[...], g1_ref), (ym_ref[...], g2_ref))):
        term = jax.nn.sigmoid(gl_ref[...].astype(F32)) * _dot(y, wb_ref[i])
        acc = term if acc is None else acc + term
    y = _dot(acc.astype(BF16), wo_ref[...])
    o_ref[...] = _residual_ln(x_ref[...], y, g_ref[...], lng_ref[...], lnb_ref[...], alpha)


def _resident_layer(stacked_shape, layer):
    tail = (0,) * (len(stacked_shape) - 1)
    return pl.BlockSpec((None,) + tuple(stacked_shape[1:]), lambda *_: (layer,) + tail,
                        pipeline_mode=pl.Buffered(1))


def _mixer_out_call(main, pool_w, pool_scale, y_sb, y_ml, w_branch, w_out, x, g, ln_g, ln_b, alpha, layer):
    b, s, d = x.shape
    tm = 256
    y_spec = pl.BlockSpec((None, tm, D_BRANCH), lambda i, j: (i, j, 0))

    def gate_spec(br):
        blk = GATE_COL // d + br
        return pl.BlockSpec((None, tm, d), lambda i, j: (i, j, blk))

    return pl.pallas_call(
        functools.partial(_mixer_out_kernel, alpha=alpha),
        grid=(b, s // tm),
        in_specs=[y_spec, _resident_layer(pool_w.shape, layer), _resident_layer(pool_scale.shape, layer),
                  y_spec, y_spec, gate_spec(0), gate_spec(1), gate_spec(2),
                  _resident_layer(w_branch.shape, layer), _resident_layer(w_out.shape, layer),
                  pl.BlockSpec((None, tm, d), lambda i, j: (i, j, 0)),
                  pl.BlockSpec((None, 1, d), lambda i, j: (i, 0, 0)),
                  pl.BlockSpec((1, d), lambda i, j: (0, 0)),
                  pl.BlockSpec((1, d), lambda i, j: (0, 0))],
        out_specs=pl.BlockSpec((None, tm, d), lambda i, j: (i, j, 0)),
        out_shape=jax.ShapeDtypeStruct((b, s, d), F32),
        scratch_shapes=[pltpu.VMEM((POOL_HALO, D_BRANCH), BF16)],
        compiler_params=_params("parallel", "arbitrary"),
        name="mixer_out_ln",
    )(main, pool_w, pool_scale, y_sb, y_ml, main, main, main, w_branch, w_out, x, g, ln_g, ln_b)


def _ffn_kernel(xa_ref, xb_ref, xh_ref, sc_ref, sh_ref, wv_ref, wg_ref, cv_ref, cg_ref, wd_ref, g_ref,
                lng_ref, lnb_ref, o_ref, h_ref, *, alpha):
    n = pl.program_id(2)
    half = xa_ref.shape[0]

    @pl.when(n == 0)
    def _():
        sc = 1.0 + sc_ref[...]
        sh = sh_ref[...]
        halo = jnp.where(pl.program_id(1) > 0, xh_ref[...] * sc + sh, 0.0)
        h_ref[0:ROW_HALO, :] = halo.astype(h_ref.dtype)
        h_ref[ROW_HALO:ROW_HALO + half, :] = (xa_ref[...] * sc + sh).astype(h_ref.dtype)
        h_ref[ROW_HALO + half:, :] = (xb_ref[...] * sc + sh).astype(h_ref.dtype)
        o_ref[...] = jnp.zeros_like(o_ref)

    h = h_ref[...]

    def conv(up, c_ref, cs):
        c = c_ref[:, cs]
        out = up * c[FFN_CONV - 1:FFN_CONV, :]
        for j in range(1, FFN_CONV):
            out = out + pltpu.roll(up, j, axis=0) * c[FFN_CONV - 1 - j:FFN_CONV - j, :]
        return out[ROW_HALO:, :]

    subs = [slice(j * FFN_SUB, (j + 1) * FFN_SUB) for j in range(wv_ref.shape[1] // FFN_SUB)]
    ups = [(_dot(h, wv_ref[:, cs]), _dot(h, wg_ref[:, cs])) for cs in subs]
    part = None
    for cs, (up_val, up_gate) in zip(subs, ups):
        val = conv(up_val, cv_ref, cs)
        gate = conv(up_gate, cg_ref, cs)
        act = (gate * jax.nn.sigmoid(gate) * val).astype(BF16)
        down = _dot(act, wd_ref[cs, :])
        part = down if part is None else part + down
    o_ref[...] += part

    @pl.when(n == pl.num_programs(2) - 1)
    def _():
        for x_ref, rows in ((xa_ref, slice(0, half)), (xb_ref, slice(half, 2 * half))):
            o_ref[rows, :] = _residual_ln(x_ref[...], o_ref[rows, :], g_ref[...], lng_ref[...], lnb_ref[...], alpha)


def _ffn_call(x, sc, sh, w_up, conv_ff, w_down, g, ln_g, ln_b, alpha, layer):
    b, s, d = x.shape
    dff = w_down.shape[1]
    tm = _pick(s, (1024, 512, 256))
    tn = _pick(dff, (512, 256))
    nff = dff // tn
    halo_blocks = tm // ROW_HALO
    return pl.pallas_call(
        functools.partial(_ffn_kernel, alpha=alpha),
        grid=(b, s // tm, nff),
        in_specs=[
            pl.BlockSpec((None, tm // 2, d), lambda i, j, k: (i, 2 * j, 0), pipeline_mode=pl.Buffered(1)),
            pl.BlockSpec((None, tm // 2, d), lambda i, j, k: (i, 2 * j + 1, 0), pipeline_mode=pl.Buffered(1)),
            pl.BlockSpec((None, ROW_HALO, d), lambda i, j, k: (i, jnp.maximum(j * halo_blocks - 1, 0), 0)),
            pl.BlockSpec((None, 1, d), lambda i, j, k: (i, 0, 0)),
            pl.BlockSpec((None, 1, d), lambda i, j, k: (i, 0, 0)),
            pl.BlockSpec((None, d, tn), lambda i, j, k: (layer, 0, k)),
            pl.BlockSpec((None, d, tn), lambda i, j, k: (layer, 0, nff + k)),
            pl.BlockSpec((None, FFN_CONV, tn), lambda i, j, k: (layer, 0, k)),
            pl.BlockSpec((None, FFN_CONV, tn), lambda i, j, k: (layer, 0, nff + k)),
            pl.BlockSpec((None, tn, d), lambda i, j, k: (layer, k, 0)),
            pl.BlockSpec((None, 1, d), lambda i, j, k: (i, 0, 0)),
            pl.BlockSpec((1, d), lambda i, j, k: (0, 0)),
            pl.BlockSpec((1, d), lambda i, j, k: (0, 0)),
        ],
        out_specs=pl.BlockSpec((None, tm, d), lambda i, j, k: (i, j, 0)),
        out_shape=jax.ShapeDtypeStruct((b, s, d), F32),
        scratch_shapes=[pltpu.VMEM((tm + ROW_HALO, d), BF16)],
        compiler_params=_params("parallel", "parallel", "arbitrary", vmem_limit_bytes=FFN_VMEM_LIMIT_BYTES),
        name="ffn_conv_gate_down_ln",
    )(x, x, x, sc, sh, w_up, w_up, conv_ff, conv_ff, w_down, g, ln_g, ln_b)


def kernel(x, c, w_ada, b_ada, w_in, conv_ml, pool_w, pool_scale, ig_bias, fg_bias, w_branch, w_out,
           w_up, conv_ff, w_down, ln_g, ln_b):
    b, s, d = x.shape
    depth = w_in.shape[0]
    alpha = (2.0 * depth) ** 0.25

    w_main = jnp.concatenate([w_in[:, :, :IF_COL], w_in[:, :, IF_COL + 2 * ML_HEADS:]], axis=2).astype(BF16)
    w_if = w_in[:, :, IF_COL:IF_COL + IF_PAD].astype(BF16)
    pool_w, w_branch, w_out, w_up, w_down = (a.astype(BF16) for a in (pool_w, w_branch, w_out, w_up, w_down))
    pool_scale = pool_scale.reshape(depth, 1, D_BRANCH)

    mod = _mod_call(c, w_ada, b_ada)
    for l in range(depth):
        sh1, sc1, g1, sh2, sc2, g2 = (mod[l][:, None, i * d:(i + 1) * d] for i in range(6))

        main, ifg = _inproj_call(x, sc1, sh1, w_main, w_if, l)
        ifg_t = jnp.swapaxes(ifg[:, :, :2 * ML_HEADS], 1, 2)
        gate_bias = jnp.concatenate([ig_bias[l], fg_bias[l]])
        bias_col = jnp.pad(gate_bias, (0, IF_PAD - 2 * ML_HEADS)).reshape(1, IF_PAD)
        bias_row = gate_bias.reshape(2 * ML_HEADS, 1)

        y_sb = _sb_call(main)
        y_ml = _mlstm_call(main, ifg, ifg_t, conv_ml[l], bias_col, bias_row)
        x = _mixer_out_call(main, pool_w, pool_scale, y_sb, y_ml, w_branch, w_out, x, g1,
                            ln_g[l, 0].reshape(1, d), ln_b[l, 0].reshape(1, d), alpha, l)
        x = _ffn_call(x, sc2, sh2, w_up, conv_ff, w_down, g2,
                      ln_g[l, 1].reshape(1, d), ln_b[l, 1].reshape(1, d), alpha, l)
    return x
```

```python
import functools

import jax
import jax.numpy as jnp
from jax import lax
from jax.experimental import pallas as pl
from jax.experimental.pallas import tpu as pltpu

F32 = jnp.float32
BF16 = jnp.bfloat16

D_BRANCH = 1024
POOL_WINDOWS = (2, 4, 8, 16)
POOL_GROUP_DIM = 256
POOL_HALO = 16
SB_HEADS = 8
SB_HEAD_DIM = 128
SB_STEP_BLOCKS = 4
ML_HEADS = 4
ML_HEAD_DIM = 256
ML_CONV = 4
ML_CONV_ROWS = 64
ML_STEP_CHUNKS = 2
FFN_CONV = 3
FFN_SUB = 512
LN_EPS = 1e-5
GATE_COL = 8 * D_BRANCH
IF_COL = 8 * D_BRANCH
IF_PAD = 128
ROW_HALO = 16
CONV_HALO = 8

LOG2_E = 1.4426950408889634

SB_ZERO_BITS = 127.0

VMEM_LIMIT_BYTES = 48 * 1024 * 1024
INPROJ_VMEM_LIMIT_BYTES = 56 * 1024 * 1024
FFN_VMEM_LIMIT_BYTES = 58 * 1024 * 1024


def _params(*sem, vmem_limit_bytes=VMEM_LIMIT_BYTES):
    return pltpu.CompilerParams(dimension_semantics=sem, vmem_limit_bytes=vmem_limit_bytes)


def _pick(n, cands):
    for c in cands:
        if n % c == 0:
            return c
    raise ValueError(f"no tile in {cands} divides {n}")


def _dot(a, b):
    return jnp.dot(a, b, preferred_element_type=F32)


def _dot_nt(a, b):
    return lax.dot_general(a, b, (((1,), (1,)), ((), ())), preferred_element_type=F32)


def _dot_tn(a, b):
    return lax.dot_general(a, b, (((0,), (0,)), ((), ())), preferred_element_type=F32)


def _log_sigmoid_parts(z):
    return jnp.log1p(jnp.exp(-jnp.abs(z)))


def _split_bf16(a, terms):
    parts = []
    rem = a
    for _ in range(terms):
        p = rem.astype(BF16)
        parts.append(p)
        rem = rem - p.astype(F32)
    return parts


def _mod_kernel(c_ref, w_ref, b_ref, o_ref):
    c = c_ref[...]
    ca = (c * jax.nn.sigmoid(c)).astype(BF16)
    o_ref[...] = _dot(ca, w_ref[...].astype(BF16)) + b_ref[...]


def _mod_call(c, w_ada, b_ada):
    depth, d, n = w_ada.shape
    b = c.shape[0]
    tn = _pick(n, (1024, 512, 256, 128))
    return pl.pallas_call(
        _mod_kernel,
        grid=(depth, n // tn),
        in_specs=[
            pl.BlockSpec((b, d), lambda l, j: (0, 0)),
            pl.BlockSpec((None, d, tn), lambda l, j: (l, 0, j)),
            pl.BlockSpec((None, 1, tn), lambda l, j: (l, 0, j)),
        ],
        out_specs=pl.BlockSpec((None, b, tn), lambda l, j: (l, 0, j)),
        out_shape=jax.ShapeDtypeStruct((depth, b, n), F32),
        compiler_params=_params("parallel", "parallel"),
        name="adaln_mod",
    )(c, w_ada, b_ada.reshape(depth, 1, n))


def _inproj_kernel(x_ref, sc_ref, sh_ref, w_ref, wif_ref, o_ref, oif_ref, h_ref):
    @pl.when(pl.program_id(2) == 0)
    def _():
        h = (x_ref[...] * (1.0 + sc_ref[...]) + sh_ref[...]).astype(h_ref.dtype)
        h_ref[...] = h
        oif_ref[...] = _dot(h, wif_ref[...])

    o_ref[...] = _dot(h_ref[...], w_ref[...]).astype(o_ref.dtype)


def _inproj_call(x, sc, sh, w, w_if, layer):
    b, s, d = x.shape
    n = w.shape[2]
    tm = _pick(s, (1024, 512, 256))
    tn = _pick(n, (2048, 1024, 512, 256, 128))
    return pl.pallas_call(
        _inproj_kernel,
        grid=(b, s // tm, n // tn),
        in_specs=[
            pl.BlockSpec((None, tm, d), lambda i, j, k: (i, j, 0)),
            pl.BlockSpec((None, 1, d), lambda i, j, k: (i, 0, 0)),
            pl.BlockSpec((None, 1, d), lambda i, j, k: (i, 0, 0)),
            pl.BlockSpec((None, d, tn), lambda i, j, k: (layer, 0, k)),
            pl.BlockSpec((None, d, IF_PAD), lambda i, j, k: (layer, 0, 0)),
        ],
        out_specs=[pl.BlockSpec((None, tm, tn), lambda i, j, k: (i, j, k)),
                   pl.BlockSpec((None, tm, IF_PAD), lambda i, j, k: (i, j, 0))],
        out_shape=[jax.ShapeDtypeStruct((b, s, n), BF16), jax.ShapeDtypeStruct((b, s, IF_PAD), F32)],
        scratch_shapes=[pltpu.VMEM((tm, d), BF16)],
        compiler_params=_params("parallel", "parallel", "arbitrary", vmem_limit_bytes=INPROJ_VMEM_LIMIT_BYTES),
        name="in_proj",
    )(x, sc, sh, w, w_if)


def _pool_tile(a_ref, w_ref, scale_ref, prev_ref, st):
    t = a_ref.shape[0]

    @pl.when(st == 0)
    def _():
        prev_ref[...] = jnp.zeros_like(prev_ref)

    row = lax.broadcasted_iota(jnp.int32, (t, t), 0)
    col = lax.broadcasted_iota(jnp.int32, (t, t), 1)
    prow = lax.broadcasted_iota(jnp.int32, (t, POOL_HALO), 0)
    pcol = lax.broadcasted_iota(jnp.int32, (t, POOL_HALO), 1) - POOL_HALO
    t_abs = st * t + lax.broadcasted_iota(jnp.int32, (t, 1), 0)
    outs = []
    for g, win in enumerate(POOL_WINDOWS):
        cs = slice(g * POOL_GROUP_DIM, (g + 1) * POOL_GROUP_DIM)
        a = a_ref[:, cs]
        band = jnp.where((col <= row) & (col > row - win), 1.0, 0.0).astype(BF16)
        pband = jnp.where(pcol > prow - win, 1.0, 0.0).astype(BF16)
        wsum = _dot(band, a) + _dot(pband, prev_ref[:, cs])
        cnt = jnp.minimum(t_abs + 1, win).astype(F32)
        diff = (wsum / cnt - a.astype(F32)).astype(BF16)
        outs.append((_dot(diff, w_ref[g]) * scale_ref[:, cs]).astype(BF16))
    prev_ref[...] = a_ref[t - POOL_HALO:, :]
    return jnp.concatenate(outs, axis=1)


def _sb_kernel(q_ref, k_ref, v_ref, o_ref, *, tq):
    scale = SB_HEAD_DIM ** -0.5 * LOG2_E
    row = lax.broadcasted_iota(jnp.int32, (tq, tq), 0)
    col = lax.broadcasted_iota(jnp.int32, (tq, tq), 1)
    below = row > col
    after_sum = jnp.where(below, 1.0, 0.0).astype(BF16)

    def sweep(qrows, rows, accs, rs):
        diag = accs is None
        heads = [slice(h * SB_HEAD_DIM, (h + 1) * SB_HEAD_DIM) for h in range(SB_HEADS)]
        zs = [_dot_nt(q_ref[qrows, hs], k_ref[rows, hs]) * scale for hs in heads]
        log_betas, drops, his, los = [], [], [], []
        for z in zs:
            l = jnp.log2(1.0 + jnp.exp2(-jnp.abs(z)))
            drop = jnp.maximum(z, 0.0) + l
            if diag:
                drop = jnp.where(below, drop, 0.0)
            hi, lo = _split_bf16(drop, 2)
            log_betas.append(jnp.minimum(z, 0.0) - l)
            drops.append(drop)
            his.append(hi)
            los.append(lo)
        sums = _dot(jnp.concatenate(his + los, axis=0), after_sum)
        attns, new_rs, rmin = [], [], None
        for h in range(SB_HEADS):
            after = sums[h * tq:(h + 1) * tq] + sums[(SB_HEADS + h) * tq:(SB_HEADS + h + 1) * tq]
            rowsum = jnp.sum(drops[h], axis=1, keepdims=True)
            if diag:
                attn = jnp.where(below, jnp.exp2(log_betas[h] - after), 0.0)
                rn = rowsum
            else:
                attn = jnp.exp2(log_betas[h] - (after + rs[h]))
                rn = rs[h] + rowsum
            attns.append(attn.astype(BF16))
            new_rs.append(rn)
            rmin = rn if rmin is None else jnp.minimum(rmin, rn)
        pvs = [_dot(attns[h], v_ref[rows, hs]) for h, hs in enumerate(heads)]
        new_accs = pvs if diag else [a + p for a, p in zip(accs, pvs)]
        return tuple(new_accs), tuple(new_rs), jnp.min(rmin)

    def cond(carry):
        kb, rmin, _, _ = carry
        return jnp.logical_and(kb >= 0, rmin < SB_ZERO_BITS)

    def query_block(i, carry):
        qb = pl.program_id(1) * SB_STEP_BLOCKS + i
        qrows = pl.ds(pl.multiple_of(i * tq, tq), tq)
        accs0, rs0, rmin0 = sweep(qrows, pl.ds(pl.multiple_of(qb * tq, tq), tq), None, None)

        def body(c):
            kb, _, accs, rs = c
            accs, rs, rmin = sweep(qrows, pl.ds(pl.multiple_of(kb * tq, tq), tq), accs, rs)
            return kb - 1, rmin, accs, rs

        _, _, accs, _ = lax.while_loop(cond, body, (qb - 1, rmin0, accs0, rs0))
        for h in range(SB_HEADS):
            o_ref[qrows, h * SB_HEAD_DIM:(h + 1) * SB_HEAD_DIM] = accs[h].astype(o_ref.dtype)
        return carry

    lax.fori_loop(0, SB_STEP_BLOCKS, query_block, 0)


def _sb_call(main):
    b, s, _ = main.shape
    tq = 128
    step = tq * SB_STEP_BLOCKS
    return pl.pallas_call(
        functools.partial(_sb_kernel, tq=tq),
        grid=(b, s // step),
        in_specs=[
            pl.BlockSpec((None, step, D_BRANCH), lambda i, j: (i, j, 1)),
            pl.BlockSpec((None, s, D_BRANCH), lambda i, j: (i, 0, 2)),
            pl.BlockSpec((None, s, D_BRANCH), lambda i, j: (i, 0, 3)),
        ],
        out_specs=pl.BlockSpec((None, step, D_BRANCH), lambda i, j: (i, j, 0)),
        out_shape=jax.ShapeDtypeStruct((b, s, D_BRANCH), BF16),
        compiler_params=_params("parallel", "arbitrary"),
        name="stick_breaking",
    )(main, main, main)


def _mlstm_kernel(qk_ref, v_ref, og_ref, if_ref, ift_ref, cw_ref, bcol_ref, brow_ref, o_ref,
                  ext_ref, qkc_ref, ct_ref, n_ref, m_ref, *, chunk):
    @pl.when(pl.program_id(1) == 0)
    def _():
        ext_ref[0:CONV_HALO, :] = jnp.zeros((CONV_HALO, ext_ref.shape[1]), F32)
        ct_ref[...] = jnp.zeros_like(ct_ref)
        n_ref[...] = jnp.zeros_like(n_ref)
        m_ref[...] = jnp.zeros_like(m_ref)

    for c in range(ML_STEP_CHUNKS):
        rows = slice(c * chunk, (c + 1) * chunk)
        _mlstm_chunk(qk_ref.at[rows, :], v_ref.at[rows, :], og_ref.at[rows, :], if_ref.at[rows, :],
                     ift_ref.at[:, rows], cw_ref, bcol_ref, brow_ref, o_ref.at[rows, :],
                     ext_ref, qkc_ref, ct_ref, n_ref, m_ref, chunk)


def _mlstm_chunk(qk_ref, v_ref, og_ref, if_ref, ift_ref, cw_ref, bcol_ref, brow_ref, o_ref,
                 ext_ref, qkc_ref, ct_ref, n_ref, m_ref, chunk):
    L = chunk

    ext_ref[CONV_HALO:, :] = qk_ref[...].astype(F32)
    for cb in range(ext_ref.shape[1] // 128):
        cs = slice(cb * 128, (cb + 1) * 128)
        cw = cw_ref[:, cs]
        for rb in range(L // ML_CONV_ROWS):
            raw = ext_ref[rb * ML_CONV_ROWS:(rb + 1) * ML_CONV_ROWS + CONV_HALO, cs]
            conv = raw * cw[ML_CONV - 1:ML_CONV, :]
            for j in range(1, ML_CONV):
                conv = conv + pltpu.roll(raw, j, axis=0) * cw[ML_CONV - 1 - j:ML_CONV - j, :]
            conv = conv[CONV_HALO:, :]
            qkc_ref[rb * ML_CONV_ROWS:(rb + 1) * ML_CONV_ROWS, cs] = conv * jax.nn.sigmoid(conv)
    ext_ref[0:CONV_HALO, :] = ext_ref[L:L + CONV_HALO, :]

    if_c = if_ref[...] + bcol_ref[...]
    if_r = ift_ref[...] + brow_ref[...]
    logf_c = jnp.minimum(if_c, 0.0) - _log_sigmoid_parts(if_c)
    logf_r = jnp.minimum(if_r, 0.0) - _log_sigmoid_parts(if_r)
    row = lax.broadcasted_iota(jnp.int32, (L, L), 0)
    col = lax.broadcasted_iota(jnp.int32, (L, L), 1)
    causal = col <= row
    incl_lower = jnp.where(causal, 1.0, 0.0).astype(BF16)
    incl_upper = jnp.where(row <= col, 1.0, 0.0).astype(BF16)
    b_c = sum(_dot(incl_lower, p) for p in _split_bf16(logf_c, 3))
    b_r = sum(_dot(p, incl_upper) for p in _split_bf16(logf_r, 3))

    for h in range(ML_HEADS):
        hs = slice(h * ML_HEAD_DIM, (h + 1) * ML_HEAD_DIM)
        ks = slice(D_BRANCH + h * ML_HEAD_DIM, D_BRANCH + (h + 1) * ML_HEAD_DIM)
        bcol = b_c[:, ML_HEADS + h:ML_HEADS + h + 1]
        icol = if_c[:, h:h + 1]
        brow = b_r[ML_HEADS + h:ML_HEADS + h + 1, :]
        irow = if_r[h:h + 1, :]
        m_prev = m_ref[h:h + 1, 0:1]

        d = jnp.where(causal, bcol - brow + irow, -jnp.inf)
        inter = bcol + m_prev
        m_t = jnp.maximum(inter, jnp.max(d, axis=1, keepdims=True))
        w_intra = jnp.exp(d - m_t)
        w_inter = jnp.exp(inter - m_t)

        q = qkc_ref[:, hs].astype(BF16)
        k32 = qkc_ref[:, ks] * (ML_HEAD_DIM ** -0.5)
        v = v_ref[:, hs]
        ct = ct_ref[h]
        nrow = n_ref[h:h + 1, :]
        s = _dot_nt(q, k32.astype(BF16)) * w_intra
        num = w_inter * _dot(q, ct.astype(BF16)) + _dot(s.astype(BF16), v)
        den = (w_inter * jnp.sum(q.astype(F32) * nrow, axis=1, keepdims=True)
               + jnp.sum(s, axis=1, keepdims=True))
        hid = num / jnp.maximum(jnp.abs(den), jnp.exp(-m_t))
        og = og_ref[:, hs].astype(F32)
        o_ref[:, hs] = (hid * jax.nn.sigmoid(og)).astype(o_ref.dtype)

        b_last = bcol[L - 1:L, :]
        dec = b_last - bcol + icol
        m_new = jnp.maximum(b_last + m_prev, jnp.max(dec, axis=0, keepdims=True))
        w_s = jnp.exp(dec - m_new)
        w_prev = jnp.exp(b_last + m_prev - m_new)
        kw = k32 * w_s
        ct_ref[h] = w_prev * ct + _dot_tn(kw.astype(BF16), v)
        n_ref[h:h + 1, :] = w_prev * nrow + jnp.sum(kw, axis=0, keepdims=True)
        m_ref[h:h + 1, :] = jnp.broadcast_to(m_new, (1, m_ref.shape[1]))


def _mlstm_call(main, ifg, ifg_t, conv_w, bias_col, bias_row):
    b, s, _ = main.shape
    chunk = 256
    step = chunk * ML_STEP_CHUNKS
    return pl.pallas_call(
        functools.partial(_mlstm_kernel, chunk=chunk),
        grid=(b, s // step),
        in_specs=[
            pl.BlockSpec((None, step, 2 * D_BRANCH), lambda i, j: (i, j, 2)),
            pl.BlockSpec((None, step, D_BRANCH), lambda i, j: (i, j, 6)),
            pl.BlockSpec((None, step, D_BRANCH), lambda i, j: (i, j, 7)),
            pl.BlockSpec((None, step, IF_PAD), lambda i, j: (i, j, 0)),
            pl.BlockSpec((None, 2 * ML_HEADS, step), lambda i, j: (i, 0, j)),
            pl.BlockSpec((ML_CONV, 2 * D_BRANCH), lambda i, j: (0, 0)),
            pl.BlockSpec((1, IF_PAD), lambda i, j: (0, 0)),
            pl.BlockSpec((2 * ML_HEADS, 1), lambda i, j: (0, 0)),
        ],
        out_specs=pl.BlockSpec((None, step, D_BRANCH), lambda i, j: (i, j, 0)),
        out_shape=jax.ShapeDtypeStruct((b, s, D_BRANCH), BF16),
        scratch_shapes=[
            pltpu.VMEM((chunk + CONV_HALO, 2 * D_BRANCH), F32),
            pltpu.VMEM((chunk, 2 * D_BRANCH), F32),
            pltpu.VMEM((ML_HEADS, ML_HEAD_DIM, ML_HEAD_DIM), F32),
            pltpu.VMEM((2 * ML_HEADS, ML_HEAD_DIM), F32),
            pltpu.VMEM((2 * ML_HEADS, 128), F32),
        ],
        compiler_params=_params("parallel", "arbitrary"),
        name="mlstm",
    )(main, main, main, ifg, ifg_t, conv_w, bias_col, bias_row)


def _residual_ln(x, y, g, ln_g, ln_b, alpha):
    r = alpha * x + (1.0 + g) * y
    mu = jnp.mean(r, axis=-1, keepdims=True)
    cen = r - mu
    var = jnp.mean(cen * cen, axis=-1, keepdims=True)
    return cen * lax.rsqrt(var + LN_EPS) * ln_g + ln_b


def _mixer_out_kernel(ap_ref, pw_ref, ps_ref, ys_ref, ym_ref, g0_ref, g1_ref, g2_ref, wb_ref, wo_ref, x_ref,
                      g_ref, lng_ref, lnb_ref, o_ref, prev_ref, *, alpha):
    y_pool = _pool_tile(ap_ref, pw_ref, ps_ref, prev_ref, pl.program_id(1))
    acc = None
    for i, (y, gl_ref) in enumerate(((y_pool, g0_ref), (ys_ref[...], g1_ref), (ym_ref[...], g2_ref))):
        term = jax.nn.sigmoid(gl_ref[...].astype(F32)) * _dot(y, wb_ref[i])
        acc = term if acc is None else acc + term
    y = _dot(acc.astype(BF16), wo_ref[...])
    o_ref[...] = _residual_ln(x_ref[...], y, g_ref[...], lng_ref[...], lnb_ref[...], alpha)


def _resident_layer(stacked_shape, layer):
    tail = (0,) * (len(stacked_shape) - 1)
    return pl.BlockSpec((None,) + tuple(stacked_shape[1:]), lambda *_: (layer,) + tail,
                        pipeline_mode=pl.Buffered(1))


def _mixer_out_call(main, pool_w, pool_scale, y_sb, y_ml, w_branch, w_out, x, g, ln_g, ln_b, alpha, layer):
    b, s, d = x.shape
    tm = 256
    y_spec = pl.BlockSpec((None, tm, D_BRANCH), lambda i, j: (i, j, 0))

    def gate_spec(br):
        blk = GATE_COL // d + br
        return pl.BlockSpec((None, tm, d), lambda i, j: (i, j, blk))

    return pl.pallas_call(
        functools.partial(_mixer_out_kernel, alpha=alpha),
        grid=(b, s // tm),
        in_specs=[y_spec, _resident_layer(pool_w.shape, layer), _resident_layer(pool_scale.shape, layer),
                  y_spec, y_spec, gate_spec(0), gate_spec(1), gate_spec(2),
                  _resident_layer(w_branch.shape, layer), _resident_layer(w_out.shape, layer),
                  pl.BlockSpec((None, tm, d), lambda i, j: (i, j, 0)),
                  pl.BlockSpec((None, 1, d), lambda i, j: (i, 0, 0)),
                  pl.BlockSpec((1, d), lambda i, j: (0, 0)),
                  pl.BlockSpec((1, d), lambda i, j: (0, 0))],
        out_specs=pl.BlockSpec((None, tm, d), lambda i, j: (i, j, 0)),
        out_shape=jax.ShapeDtypeStruct((b, s, d), F32),
        scratch_shapes=[pltpu.VMEM((POOL_HALO, D_BRANCH), BF16)],
        compiler_params=_params("parallel", "arbitrary"),
        name="mixer_out_ln",
    )(main, pool_w, pool_scale, y_sb, y_ml, main, main, main, w_branch, w_out, x, g, ln_g, ln_b)


def _ffn_kernel(x_ref, xh_ref, sc_ref, sh_ref, wv_ref, wg_ref, cv_ref, cg_ref, wd_ref, g_ref,
                lng_ref, lnb_ref, o_ref, h_ref, *, alpha):
    n = pl.program_id(2)

    @pl.when(n == 0)
    def _():
        sc = 1.0 + sc_ref[...]
        sh = sh_ref[...]
        halo = jnp.where(pl.program_id(1) > 0, xh_ref[...] * sc + sh, 0.0)
        h_ref[0:ROW_HALO, :] = halo.astype(h_ref.dtype)
        h_ref[ROW_HALO:, :] = (x_ref[...] * sc + sh).astype(h_ref.dtype)
        o_ref[...] = jnp.zeros_like(o_ref)

    h = h_ref[...]

    def conv(up, c_ref, cs):
        c = c_ref[:, cs]
        out = up * c[FFN_CONV - 1:FFN_CONV, :]
        for j in range(1, FFN_CONV):
            out = out + pltpu.roll(up, j, axis=0) * c[FFN_CONV - 1 - j:FFN_CONV - j, :]
        return out[ROW_HALO:, :]

    subs = [slice(j * FFN_SUB, (j + 1) * FFN_SUB) for j in range(wv_ref.shape[1] // FFN_SUB)]
    ups = [(_dot(h, wv_ref[:, cs]), _dot(h, wg_ref[:, cs])) for cs in subs]
    part = None
    for cs, (up_val, up_gate) in zip(subs, ups):
        val = conv(up_val, cv_ref, cs)
        gate = conv(up_gate, cg_ref, cs)
        act = (gate * jax.nn.sigmoid(gate) * val).astype(BF16)
        down = _dot(act, wd_ref[cs, :])
        part = down if part is None else part + down
    o_ref[...] += part

    @pl.when(n == pl.num_programs(2) - 1)
    def _():
        o_ref[...] = _residual_ln(x_ref[...], o_ref[...], g_ref[...], lng_ref[...], lnb_ref[...], alpha)


def _ffn_call(x, sc, sh, w_up, conv_ff, w_down, g, ln_g, ln_b, alpha, layer):
    b, s, d = x.shape
    dff = w_down.shape[1]
    tm = _pick(s, (1024, 512, 256))
    tn = _pick(dff, (512, 256))
    nff = dff // tn
    halo_blocks = tm // ROW_HALO
    return pl.pallas_call(
        functools.partial(_ffn_kernel, alpha=alpha),
        grid=(b, s // tm, nff),
        in_specs=[
            pl.BlockSpec((None, tm, d), lambda i, j, k: (i, j, 0), pipeline_mode=pl.Buffered(1)),
            pl.BlockSpec((None, ROW_HALO, d), lambda i, j, k: (i, jnp.maximum(j * halo_blocks - 1, 0), 0)),
            pl.BlockSpec((None, 1, d), lambda i, j, k: (i, 0, 0)),
            pl.BlockSpec((None, 1, d), lambda i, j, k: (i, 0, 0)),
            pl.BlockSpec((None, d, tn), lambda i, j, k: (layer, 0, k)),
            pl.BlockSpec((None, d, tn), lambda i, j, k: (layer, 0, nff + k)),
            pl.BlockSpec((None, FFN_CONV, tn), lambda i, j, k: (layer, 0, k)),
            pl.BlockSpec((None, FFN_CONV, tn), lambda i, j, k: (layer, 0, nff + k)),
            pl.BlockSpec((None, tn, d), lambda i, j, k: (layer, k, 0)),
            pl.BlockSpec((None, 1, d), lambda i, j, k: (i, 0, 0)),
            pl.BlockSpec((1, d), lambda i, j, k: (0, 0)),
            pl.BlockSpec((1, d), lambda i, j, k: (0, 0)),
        ],
        out_specs=pl.BlockSpec((None, tm, d), lambda i, j, k: (i, j, 0)),
        out_shape=jax.ShapeDtypeStruct((b, s, d), F32),
        scratch_shapes=[pltpu.VMEM((tm + ROW_HALO, d), BF16)],
        compiler_params=_params("parallel", "parallel", "arbitrary", vmem_limit_bytes=FFN_VMEM_LIMIT_BYTES),
        name="ffn_conv_gate_down_ln",
    )(x, x, sc, sh, w_up, w_up, conv_ff, conv_ff, w_down, g, ln_g, ln_b)


def kernel(x, c, w_ada, b_ada, w_in, conv_ml, pool_w, pool_scale, ig_bias, fg_bias, w_branch, w_out,
           w_up, conv_ff, w_down, ln_g, ln_b):
    b, s, d = x.shape
    depth = w_in.shape[0]
    alpha = (2.0 * depth) ** 0.25

    w_main = jnp.concatenate([w_in[:, :, :IF_COL], w_in[:, :, IF_COL + 2 * ML_HEADS:]], axis=2).astype(BF16)
    w_if = w_in[:, :, IF_COL:IF_COL + IF_PAD].astype(BF16)
    pool_w, w_branch, w_out, w_up, w_down = (a.astype(BF16) for a in (pool_w, w_branch, w_out, w_up, w_down))
    pool_scale = pool_scale.reshape(depth, 1, D_BRANCH)

    mod = _mod_call(c, w_ada, b_ada)
    for l in range(depth):
        sh1, sc1, g1, sh2, sc2, g2 = (mod[l][:, None, i * d:(i + 1) * d] for i in range(6))

        main, ifg = _inproj_call(x, sc1, sh1, w_main, w_if, l)
        ifg_t = jnp.swapaxes(ifg[:, :, :2 * ML_HEADS], 1, 2)
        gate_bias = jnp.concatenate([ig_bias[l], fg_bias[l]])
        bias_col = jnp.pad(gate_bias, (0, IF_PAD - 2 * ML_HEADS)).reshape(1, IF_PAD)
        bias_row = gate_bias.reshape(2 * ML_HEADS, 1)

        y_sb = _sb_call(main)
        y_ml = _mlstm_call(main, ifg, ifg_t, conv_ml[l], bias_col, bias_row)
        x = _mixer_out_call(main, pool_w, pool_scale, y_sb, y_ml, w_branch, w_out, x, g1,
                            ln_g[l, 0].reshape(1, d), ln_b[l, 0].reshape(1, d), alpha, l)
        x = _ffn_call(x, sc2, sh2, w_up, conv_ff, w_down, g2,
                      ln_g[l, 1].reshape(1, d), ln_b[l, 1].reshape(1, d), alpha, l)
    return x
```
